```python
import math
import jax, jax.numpy as jnp
from jax import lax
import numpy as np

D_MODEL = 1024
BATCH = 16
SEQ = 2048
DEPTH = 2

MEM_LEN = 256
D_MIX = D_MODEL
D_SSM = (3 * D_MIX) // 8
D_POOL = D_MIX // 4
D_CONV = D_MIX - D_SSM - D_POOL
SSM_GROUP = 16
N_SSM_GROUPS = D_SSM // SSM_GROUP
SSM_STATE = 64
POOL_WINDOWS = (2, 4, 8, 16)
N_POOL_GROUPS = len(POOL_WINDOWS)
POOL_GROUP = D_POOL // N_POOL_GROUPS
CONV_WIDTH = 31
D_IN = D_SSM + D_POOL + 2 * D_CONV
D_FF = 2816
N_XHEADS = 4
XHEAD_DIM = D_MODEL // N_XHEADS
EPS = 1e-6
DT_MIN = 1e-3
DT_MAX = 1e-1

kernel_name = "hybrid_s5_pool_conv_macaron_xattn"


def rmsnorm(x, g):
    xf = x.astype(jnp.float32)
    y = xf * lax.rsqrt(jnp.mean(xf * xf, axis=-1, keepdims=True) + EPS)
    return (y * g.astype(jnp.float32)).astype(x.dtype)


def swiglu_ffn(h, w_gate, w_up, w_down):
    return (jax.nn.silu(h @ w_gate) * (h @ w_up)) @ w_down


def _complex_linear_combine(c1, c2):
    ar1, ai1, br1, bi1 = c1
    ar2, ai2, br2, bi2 = c2
    ar = ar2 * ar1 - ai2 * ai1
    ai = ar2 * ai1 + ai2 * ar1
    br = ar2 * br1 - ai2 * bi1 + br2
    bi = ar2 * bi1 + ai2 * br1 + bi2
    return (ar, ai, br, bi)


def s5_mixer(u, lam_re, lam_im, log_dt, b_re, b_im, c_re, c_im, d, w_glu):
    bsz, s, _ = u.shape
    f32 = jnp.float32
    uf = u.astype(f32)
    ug = uf.reshape(bsz, s, N_SSM_GROUPS, SSM_GROUP)
    lr = lam_re.astype(f32)
    li = lam_im.astype(f32)
    dt = jnp.exp(log_dt.astype(f32))[:, None]
    mag = jnp.exp(lr * dt)
    ar = mag * jnp.cos(li * dt)
    ai = mag * jnp.sin(li * dt)
    den = lr * lr + li * li
    zr = ((ar - 1.0) * lr + ai * li) / den
    zi = (ai * lr - (ar - 1.0) * li) / den
    br_ = b_re.astype(f32)
    bi_ = b_im.astype(f32)
    bbar_r = zr[..., None] * br_ - zi[..., None] * bi_
    bbar_i = zr[..., None] * bi_ + zi[..., None] * br_
    bu_r = jnp.einsum('gpk,bsgk->bsgp', bbar_r, ug)
    bu_i = jnp.einsum('gpk,bsgk->bsgp', bbar_i, ug)
    a_r = jnp.broadcast_to(ar[None, None], (1, s, N_SSM_GROUPS, SSM_STATE))
    a_i = jnp.broadcast_to(ai[None, None], (1, s, N_SSM_GROUPS, SSM_STATE))
    _, _, x_r, x_i = lax.associative_scan(_complex_linear_combine, (a_r, a_i, bu_r, bu_i), axis=1)
    y = (jnp.einsum('gkp,bsgp->bsgk', c_re.astype(f32), x_r)
         - jnp.einsum('gkp,bsgp->bsgk', c_im.astype(f32), x_i))
    y = y.reshape(bsz, s, D_SSM) + d.astype(f32) * uf
    y = jax.nn.gelu(y)
    out = y * jax.nn.sigmoid(y @ w_glu.astype(f32))
    return out.astype(u.dtype)


def pool_mixer(u, w_pool, pool_scale):
    bsz, s, _ = u.shape
    uf = u.astype(jnp.float32)
    cs = jnp.cumsum(uf, axis=1)
    pos = jnp.arange(1, s + 1, dtype=jnp.float32)[None, :, None]
    outs = []
    for gi, w in enumerate(POOL_WINDOWS):
        c = cs[..., gi * POOL_GROUP:(gi + 1) * POOL_GROUP]
        prev = jnp.pad(c[:, :-w], ((0, 0), (w, 0), (0, 0)))
        mean = (c - prev) / jnp.minimum(pos, float(w))
        outs.append(mean - uf[..., gi * POOL_GROUP:(gi + 1) * POOL_GROUP])
    p = jnp.stack(outs, axis=2)
    p = jnp.einsum('bsgc,gcd->bsgd', p, w_pool.astype(jnp.float32)).reshape(bsz, s, D_POOL)
    return (p * pool_scale.astype(jnp.float32)).astype(u.dtype)


def conv_module(v, g, conv_w, conv_b, ln_g, ln_b):
    h = v * jax.nn.sigmoid(g)
    h = lax.conv_general_dilated(
        h, conv_w[:, None, :].astype(h.dtype), window_strides=(1,),
        padding=[(CONV_WIDTH - 1, 0)], dimension_numbers=('NWC', 'WIO', 'NWC'),
        feature_group_count=D_CONV) + conv_b
    hf = h.astype(jnp.float32)
    mu = jnp.mean(hf, axis=-1, keepdims=True)
    var = jnp.mean(jnp.square(hf - mu), axis=-1, keepdims=True)
    hf = (hf - mu) * lax.rsqrt(var + EPS) * ln_g.astype(jnp.float32) + ln_b.astype(jnp.float32)
    return jax.nn.silu(hf).astype(v.dtype)


def cross_attention(h, m, wq, wk, wv, wo):
    bsz, s, _ = h.shape
    mlen = m.shape[1]
    q = (h @ wq).reshape(bsz, s, N_XHEADS, XHEAD_DIM)
    k = (m @ wk).reshape(bsz, mlen, N_XHEADS, XHEAD_DIM)
    v = (m @ wv).reshape(bsz, mlen, N_XHEADS, XHEAD_DIM)
    scores = jnp.einsum('bshd,bmhd->bhsm', q, k).astype(jnp.float32) * (XHEAD_DIM ** -0.5)
    probs = jax.nn.softmax(scores, axis=-1).astype(v.dtype)
    o = jnp.einsum('bhsm,bmhd->bshd', probs, v).reshape(bsz, s, D_MODEL)
    return o @ wo


def setup_inputs(seed: int = 0) -> dict:
    key = jax.random.key(seed)
    ks = iter(jax.random.split(key, 48))
    L = DEPTH

    def nrm(shape, scale):
        return jax.random.normal(next(ks), shape, jnp.float32) * scale

    def gain(shape):
        return 1.0 + 0.05 * jax.random.normal(next(ks), shape, jnp.float32)

    inp = {}
    inp["x"] = nrm((BATCH, SEQ, D_MODEL), 1.0)
    inp["mem"] = nrm((BATCH, MEM_LEN, D_MODEL), 1.0)
    inp["ffn1_norm"] = gain((L, D_MODEL))
    inp["ffn1_w_gate"] = nrm((L, D_MODEL, D_FF), D_MODEL ** -0.5)
    inp["ffn1_w_up"] = nrm((L, D_MODEL, D_FF), D_MODEL ** -0.5)
    inp["ffn1_w_down"] = nrm((L, D_FF, D_MODEL), D_FF ** -0.5)
    inp["mix_norm"] = gain((L, D_MODEL))
    inp["w_in"] = nrm((L, D_MODEL, D_IN), D_MODEL ** -0.5)
    inp["w_out"] = nrm((L, D_MIX, D_MODEL), D_MIX ** -0.5)
    inp["ssm_lambda_re"] = -0.5 + 0.01 * jax.random.normal(next(ks), (L, N_SSM_GROUPS, SSM_STATE), jnp.float32)
    inp["ssm_lambda_im"] = jnp.broadcast_to(
        jnp.pi * jnp.arange(SSM_STATE, dtype=jnp.float32), (L, N_SSM_GROUPS, SSM_STATE))
    inp["ssm_log_dt"] = jax.random.uniform(next(ks), (L, N_SSM_GROUPS), jnp.float32,
                                           math.log(DT_MIN), math.log(DT_MAX))
    bscale = (2.0 * SSM_GROUP) ** -0.5
    cscale = (2.0 * SSM_STATE) ** -0.5
    inp["ssm_b_re"] = nrm((L, N_SSM_GROUPS, SSM_STATE, SSM_GROUP), bscale)
    inp["ssm_b_im"] = nrm((L, N_SSM_GROUPS, SSM_STATE, SSM_GROUP), bscale)
    inp["ssm_c_re"] = nrm((L, N_SSM_GROUPS, SSM_GROUP, SSM_STATE), cscale)
    inp["ssm_c_im"] = nrm((L, N_SSM_GROUPS, SSM_GROUP, SSM_STATE), cscale)
    inp["ssm_d"] = nrm((L, D_SSM), 1.0)
    inp["ssm_w_glu"] = nrm((L, D_SSM, D_SSM), D_SSM ** -0.5)
    inp["pool_w"] = nrm((L, N_POOL_GROUPS, POOL_GROUP, POOL_GROUP), POOL_GROUP ** -0.5)
    inp["pool_scale"] = gain((L, D_POOL))
    inp["conv_w"] = nrm((L, CONV_WIDTH, D_CONV), CONV_WIDTH ** -0.5)
    inp["conv_b"] = nrm((L, D_CONV), 0.02)
    inp["conv_ln_g"] = gain((L, D_CONV))
    inp["conv_ln_b"] = nrm((L, D_CONV), 0.02)
    inp["xattn_norm"] = gain((L, D_MODEL))
    inp["mem_norm"] = gain((L, D_MODEL))
    inp["xattn_wq"] = nrm((L, D_MODEL, D_MODEL), D_MODEL ** -0.5)
    inp["xattn_wk"] = nrm((L, D_MODEL, D_MODEL), D_MODEL ** -0.5)
    inp["xattn_wv"] = nrm((L, D_MODEL, D_MODEL), D_MODEL ** -0.5)
    inp["xattn_wo"] = nrm((L, D_MODEL, D_MODEL), D_MODEL ** -0.5)
    inp["ffn2_norm"] = gain((L, D_MODEL))
    inp["ffn2_w_gate"] = nrm((L, D_MODEL, D_FF), D_MODEL ** -0.5)
    inp["ffn2_w_up"] = nrm((L, D_MODEL, D_FF), D_MODEL ** -0.5)
    inp["ffn2_w_down"] = nrm((L, D_FF, D_MODEL), D_FF ** -0.5)
    inp["final_norm"] = gain((D_MODEL,))
    return inp


def reference(x, mem, ffn1_norm, ffn1_w_gate, ffn1_w_up, ffn1_w_down, mix_norm, w_in, w_out,
              ssm_lambda_re, ssm_lambda_im, ssm_log_dt, ssm_b_re, ssm_b_im, ssm_c_re, ssm_c_im,
              ssm_d, ssm_w_glu, pool_w, pool_scale, conv_w, conv_b, conv_ln_g, conv_ln_b,
              xattn_norm, mem_norm, xattn_wq, xattn_wk, xattn_wv, xattn_wo,
              ffn2_norm, ffn2_w_gate, ffn2_w_up, ffn2_w_down, final_norm):
    split_pts = [D_SSM, D_SSM + D_POOL, D_SSM + D_POOL + D_CONV]
    for l in range(DEPTH):
        h = rmsnorm(x, ffn1_norm[l])
        x = x + 0.5 * swiglu_ffn(h, ffn1_w_gate[l], ffn1_w_up[l], ffn1_w_down[l])
        h = rmsnorm(x, mix_norm[l])
        z = h @ w_in[l]
        u_ssm, u_pool, v_conv, g_conv = jnp.split(z, split_pts, axis=-1)
        y_ssm = s5_mixer(u_ssm, ssm_lambda_re[l], ssm_lambda_im[l], ssm_log_dt[l],
                         ssm_b_re[l], ssm_b_im[l], ssm_c_re[l], ssm_c_im[l],
                         ssm_d[l], ssm_w_glu[l])
        y_pool = pool_mixer(u_pool, pool_w[l], pool_scale[l])
        y_conv = conv_module(v_conv, g_conv, conv_w[l], conv_b[l], conv_ln_g[l], conv_ln_b[l])
        y = jnp.concatenate([y_ssm, y_pool, y_conv], axis=-1)
        x = x + y @ w_out[l]
        h = rmsnorm(x, xattn_norm[l])
        m = rmsnorm(mem, mem_norm[l])
        x = x + cross_attention(h, m, xattn_wq[l], xattn_wk[l], xattn_wv[l], xattn_wo[l])
        h = rmsnorm(x, ffn2_norm[l])
        x = x + 0.5 * swiglu_ffn(h, ffn2_w_gate[l], ffn2_w_up[l], ffn2_w_down[l])
    return rmsnorm(x, final_norm)
```

```python
import functools

import jax
import jax.numpy as jnp
import numpy as np
from jax import lax
from jax.experimental import pallas as pl
from jax.experimental.pallas import tpu as pltpu

F32 = jnp.float32
BF16 = jnp.bfloat16

D_MODEL = 1024
MEM_LEN = 256
D_SSM = 384
D_POOL = 256
D_CONV = 384
SSM_GROUP = 16
N_SSM_GROUPS = 24
SSM_STATE = 64
POOL_WINDOWS = (2, 4, 8, 16)
POOL_GROUP = 64
CONV_WIDTH = 31
D_IN = D_SSM + D_POOL + 2 * D_CONV
D_FF = 2816
N_XHEADS = 4
XHEAD_DIM = 256
EPS = 1e-6

LANES = 128
MXU_TILE = 256

FF_CHUNK = MXU_TILE
S5_CHUNK = 64
S5_SUB = MXU_TILE // SSM_GROUP
S5_NB = S5_CHUNK // S5_SUB
S5_ROWS = S5_CHUNK * SSM_GROUP
CONV_PAD = 32
POOL_PAD = 16
VMEM_LIMIT = 56 * 1024 * 1024


def _cparams(*sem):
    return pltpu.CompilerParams(dimension_semantics=sem, vmem_limit_bytes=VMEM_LIMIT)


def _const_spec(shape):
    nd = len(shape)
    return pl.BlockSpec(shape, lambda *_: (0,) * nd)


def _rms(x, g):
    ms = jnp.mean(x * x, axis=-1, keepdims=True)
    return x * lax.rsqrt(ms + EPS) * g


def _ffn_kernel(x_ref, g_ref, wg_ref, wu_ref, wd_ref, fg_ref, o_ref, *, final):
    x = x_ref[...]
    h = _rms(x, g_ref[...]).astype(BF16)
    acc = jnp.zeros(x.shape, F32)
    for c in range(D_FF // FF_CHUNK):
        sl = slice(c * FF_CHUNK, (c + 1) * FF_CHUNK)
        gate = jnp.dot(h, wg_ref[:, sl], preferred_element_type=F32)
        up = jnp.dot(h, wu_ref[:, sl], preferred_element_type=F32)
        act = (gate * jax.nn.sigmoid(gate) * up).astype(BF16)
        acc = acc + jnp.dot(act, wd_ref[sl, :], preferred_element_type=F32)
    y = x + 0.5 * acc
    if final:
        y = _rms(y, fg_ref[...])
    o_ref[...] = y


def _ffn(x2d, norm_g, wg, wu, wd, final_g, *, final, tm):
    n = x2d.shape[0]
    return pl.pallas_call(
        functools.partial(_ffn_kernel, final=final),
        out_shape=jax.ShapeDtypeStruct((n, D_MODEL), F32),
        grid=(n // tm,),
        in_specs=[
            pl.BlockSpec((tm, D_MODEL), lambda i: (i, 0)),
            _const_spec((1, D_MODEL)),
            _const_spec((D_MODEL, D_FF)),
            _const_spec((D_MODEL, D_FF)),
            _const_spec((D_FF, D_MODEL)),
            _const_spec((1, D_MODEL)),
        ],
        out_specs=pl.BlockSpec((tm, D_MODEL), lambda i: (i, 0)),
        compiler_params=_cparams("parallel"),
    )(x2d, norm_g, wg, wu, wd, final_g)


def _mix_in_kernel(x_ref, g_ref, w_ref, us_ref, up_ref, hc_ref):
    h = _rms(x_ref[...], g_ref[...]).astype(BF16)
    z = jnp.dot(h, w_ref[...], preferred_element_type=F32)
    us_ref[...] = z[:, :D_SSM]
    up_ref[...] = z[:, D_SSM:D_SSM + D_POOL]
    v = z[:, D_SSM + D_POOL:D_SSM + D_POOL + D_CONV]
    g = z[:, D_SSM + D_POOL + D_CONV:]
    hc_ref[...] = v * jax.nn.sigmoid(g)


def _mix_in(x2d, norm_g, w_in, *, tm):
    n = x2d.shape[0]
    return pl.pallas_call(
        _mix_in_kernel,
        out_shape=(jax.ShapeDtypeStruct((n, D_SSM), F32),
                   jax.ShapeDtypeStruct((n, D_POOL), F32),
                   jax.ShapeDtypeStruct((n, D_CONV), F32)),
        grid=(n // tm,),
        in_specs=[
            pl.BlockSpec((tm, D_MODEL), lambda i: (i, 0)),
            _const_spec((1, D_MODEL)),
            _const_spec((D_MODEL, D_IN)),
        ],
        out_specs=(pl.BlockSpec((tm, D_SSM), lambda i: (i, 0)),
                   pl.BlockSpec((tm, D_POOL), lambda i: (i, 0)),
                   pl.BlockSpec((tm, D_CONV), lambda i: (i, 0))),
        compiler_params=_cparams("parallel"),
    )(x2d, norm_g, w_in)


def _s5_prep_kernel(lr2_ref, li2_ref, ldt_ref, br2_ref, bi2_ref, lrc_ref, lic_ref, crt_ref, cit_ref,
                    zt_ref, wst_ref, gt_ref, cp_ref, *, n_chunks):
    hi = lax.Precision.HIGHEST
    lane = lax.broadcasted_iota(jnp.int32, (1, LANES), 1)
    first = lane < SSM_STATE

    lr2 = lr2_ref[0]
    li2 = li2_ref[0]
    dt = jnp.exp(ldt_ref[0])
    rho = lr2 * dt
    th = li2 * dt

    def pw_row(e):
        mag = jnp.exp(e * rho)
        return mag * jnp.cos(e * th), mag * jnp.sin(e * th)

    one = jnp.ones((1, 1), F32)
    a_r, a_i = pw_row(one)
    den = lr2 * lr2 + li2 * li2
    z_r = ((a_r - 1.0) * lr2 + a_i * li2) / den
    z_i = (a_i * lr2 - (a_r - 1.0) * li2) / den
    br2 = br2_ref[0]
    bi2 = bi2_ref[0]
    b1 = jnp.where(first, br2, bi2)
    b2 = jnp.where(first, -bi2, br2)
    y1 = z_r * b1 + z_i * b2
    y2 = z_r * b2 - z_i * b1

    e_lo = (S5_SUB - 1 - lax.broadcasted_iota(jnp.int32, (S5_SUB, 1), 0)).astype(F32)
    lo_r, lo_i = pw_row(e_lo)
    lob1 = jnp.concatenate([lo_r[i:i + 1] * y1 + lo_i[i:i + 1] * y2 for i in range(S5_SUB)], axis=0)
    lob2 = jnp.concatenate([lo_r[i:i + 1] * y2 - lo_i[i:i + 1] * y1 for i in range(S5_SUB)], axis=0)
    e_hi = (S5_SUB * (S5_NB - 1 - lax.broadcasted_iota(jnp.int32, (S5_NB, 1), 0))).astype(F32)
    hi_r, hi_i = pw_row(e_hi)
    pb = jnp.concatenate([hi_r[m:m + 1] * lob1 + hi_i[m:m + 1] * lob2 for m in range(S5_NB)], axis=0)
    wst_ref[0] = pb.astype(BF16)

    row = lax.broadcasted_iota(jnp.int32, (SSM_GROUP, 1), 0)
    last_step = max(n_chunks.bit_length() - 2, 0)
    e_cp = (S5_CHUNK * jnp.left_shift(1, jnp.minimum(row // 2, last_step))).astype(F32)
    cp_r, cp_i = pw_row(e_cp)
    cp_ref[0] = jnp.where(row % 2 == 0, cp_r, jnp.where(first, -cp_i, cp_i))

    lrc = lrc_ref[0]
    lic = lic_ref[0]
    rho_c = lrc * dt
    th_c = lic * dt
    mag_c = jnp.exp(rho_c)
    ac_r = mag_c * jnp.cos(th_c)
    ac_i = mag_c * jnp.sin(th_c)
    rc_r, rc_i = [], []
    for half in range(MXU_TILE // LANES):
        e = (lane // SSM_GROUP + half * (LANES // SSM_GROUP)).astype(F32)
        mag = jnp.exp(e * rho_c)
        t_r = mag * jnp.cos(e * th_c)
        t_i = mag * jnp.sin(e * th_c)
        c_r = crt_ref[0, :, half * LANES:(half + 1) * LANES]
        c_i = cit_ref[0, :, half * LANES:(half + 1) * LANES]
        rc_r.append(t_r * c_r - t_i * c_i)
        rc_i.append(t_r * c_i + t_i * c_r)
    rc_stack = jnp.concatenate([jnp.concatenate(rc_r, axis=1), -jnp.concatenate(rc_i, axis=1)], axis=0)
    ct_stack = jnp.concatenate([crt_ref[0], -cit_ref[0]], axis=0)

    a16_r, a16_i = ac_r, ac_i
    for _ in range(4):
        a16_r, a16_i = a16_r * a16_r - a16_i * a16_i, 2.0 * a16_r * a16_i
    g_r, g_i = ac_r, ac_i
    for j in range(S5_NB):
        for half in range(MXU_TILE // LANES):
            lo = j * MXU_TILE + half * LANES
            gt_ref[0, :SSM_STATE, lo:lo + LANES] = (g_r * rc_r[half] - g_i * rc_i[half]).astype(BF16)
            gt_ref[0, SSM_STATE:, lo:lo + LANES] = (-(g_r * rc_i[half] + g_i * rc_r[half])).astype(BF16)
        g_r, g_i = g_r * a16_r - g_i * a16_i, g_r * a16_i + g_i * a16_r

    off_rows = (S5_CHUNK - S5_SUB) * SSM_GROUP
    lo_row = (S5_SUB - 1) * SSM_GROUP
    zt_off = jnp.dot(pb[lo_row:lo_row + off_rows], rc_stack, precision=hi, preferred_element_type=F32)
    zt_ref[0, :off_rows, :] = zt_off.astype(BF16)
    kall = jnp.dot(pb[off_rows:], ct_stack, precision=hi, preferred_element_type=F32)
    colblk = lax.broadcasted_iota(jnp.int32, (1, MXU_TILE), 1) // SSM_GROUP
    zd = jnp.zeros((MXU_TILE, MXU_TILE), F32)
    for jj in range(S5_SUB):
        sh = (S5_SUB - 1 - jj) * SSM_GROUP
        shifted = kall if sh == 0 else jnp.concatenate(
            [kall[sh:], jnp.zeros((sh, MXU_TILE), F32)], axis=0)
        zd = jnp.where(colblk == jj, shifted, zd)
    zt_ref[0, off_rows:, :] = zd.astype(BF16)


def _s5_prep(lam_re, lam_im, log_dt, b_re, b_im, c_re, c_im, *, n_chunks):
    g = N_SSM_GROUPS
    lr2 = jnp.tile(lam_re, (1, 2))[:, None, :]
    li2 = jnp.tile(lam_im, (1, 2))[:, None, :]
    ldt = jnp.broadcast_to(log_dt[:, None, None], (g, 1, LANES))
    br2 = jnp.tile(jnp.swapaxes(b_re, 1, 2), (1, 1, 2))
    bi2 = jnp.tile(jnp.swapaxes(b_im, 1, 2), (1, 1, 2))
    lrc = jnp.broadcast_to(lam_re[:, :, None], (g, SSM_STATE, LANES))
    lic = jnp.broadcast_to(lam_im[:, :, None], (g, SSM_STATE, LANES))
    crt = jnp.tile(jnp.swapaxes(c_re, 1, 2), (1, 1, S5_SUB))
    cit = jnp.tile(jnp.swapaxes(c_im, 1, 2), (1, 1, S5_SUB))

    def spec(*shape):
        return pl.BlockSpec((1,) + shape, lambda i: (i, 0, 0))

    return pl.pallas_call(
        functools.partial(_s5_prep_kernel, n_chunks=n_chunks),
        out_shape=(jax.ShapeDtypeStruct((g, S5_ROWS, MXU_TILE), BF16),
                   jax.ShapeDtypeStruct((g, S5_ROWS, LANES), BF16),
                   jax.ShapeDtypeStruct((g, LANES, S5_ROWS), BF16),
                   jax.ShapeDtypeStruct((g, SSM_GROUP, LANES), F32)),
        grid=(g,),
        in_specs=[spec(1, LANES), spec(1, LANES), spec(1, LANES),
                  spec(SSM_GROUP, LANES), spec(SSM_GROUP, LANES),
                  spec(SSM_STATE, LANES), spec(SSM_STATE, LANES),
                  spec(SSM_STATE, MXU_TILE), spec(SSM_STATE, MXU_TILE)],
        out_specs=(spec(S5_ROWS, MXU_TILE), spec(S5_ROWS, LANES), spec(LANES, S5_ROWS),
                   spec(SSM_GROUP, LANES)),
        compiler_params=_cparams("parallel"),
    )(lr2, li2, ldt, br2, bi2, lrc, lic, crt, cit)


def _s5_main_kernel(ut_ref, zt_ref, wst_ref, gt_ref, cp_ref, o_ref, *, n_chunks):
    ncols = ut_ref.shape[1]
    state = jnp.dot(ut_ref[0], wst_ref[0], preferred_element_type=F32)
    chunk = lax.broadcasted_iota(jnp.int32, (ncols, LANES), 0) % n_chunks
    r = 0
    while (1 << r) < n_chunks:
        s = 1 << r
        prev = jnp.where(chunk >= s, pltpu.roll(state, s, 0), 0.0)
        state = state + cp_ref[0, 2 * r:2 * r + 1, :] * prev \
            + cp_ref[0, 2 * r + 1:2 * r + 2, :] * pltpu.roll(prev, SSM_STATE, 1)
        r += 1
    carry = jnp.where(chunk >= 1, pltpu.roll(state, 1, 0), 0.0).astype(BF16)
    for j in range(S5_NB):
        k = (j + 1) * MXU_TILE
        y = jnp.dot(ut_ref[0, :, :k], zt_ref[0, (S5_NB - 1 - j) * MXU_TILE:, :], preferred_element_type=F32)
        y = y + jnp.dot(carry, gt_ref[0, :, j * MXU_TILE:(j + 1) * MXU_TILE], preferred_element_type=F32)
        o_ref[0, :, j * MXU_TILE:(j + 1) * MXU_TILE] = y


def _s5_main(ut, zt, wst, gt, cp, *, n_chunks):
    g, ncols, _ = ut.shape

    def spec(*shape):
        return pl.BlockSpec((1,) + shape, lambda i: (i, 0, 0))

    return pl.pallas_call(
        functools.partial(_s5_main_kernel, n_chunks=n_chunks),
        out_shape=jax.ShapeDtypeStruct((g, ncols, S5_ROWS), F32),
        grid=(g,),
        in_specs=[spec(ncols, S5_ROWS), spec(S5_ROWS, MXU_TILE), spec(S5_ROWS, LANES),
                  spec(LANES, S5_ROWS), spec(SSM_GROUP, LANES)],
        out_specs=spec(ncols, S5_ROWS),
        compiler_params=_cparams("parallel"),
    )(ut, zt, wst, gt, cp)


def _pool_conv_kernel(up_ref, hc_ref, pw_ref, ps_ref, cw_ref, cb_ref, lg_ref, lb_ref, o_ref,
                      uwin_ref, hwin_ref, *, tt):
    j = pl.program_id(1)

    @pl.when(j == 0)
    def _():
        uwin_ref[:POOL_PAD, :] = jnp.zeros((POOL_PAD, D_POOL), F32)
        hwin_ref[:CONV_PAD, :] = jnp.zeros((CONV_PAD, D_CONV), F32)

    @pl.when(j > 0)
    def _():
        uwin_ref[:POOL_PAD, :] = uwin_ref[tt:, :]
        hwin_ref[:CONV_PAD, :] = hwin_ref[tt:, :]

    uwin_ref[POOL_PAD:, :] = up_ref[0]
    hwin_ref[CONV_PAD:, :] = hc_ref[0]

    lane = lax.broadcasted_iota(jnp.int32, (1, D_POOL), 1)
    win = jnp.left_shift(2, lane // POOL_GROUP)
    cur = uwin_ref[POOL_PAD:, :]
    run = cur
    sums = {}
    for d in range(1, POOL_WINDOWS[-1]):
        run = run + uwin_ref[POOL_PAD - d:POOL_PAD - d + tt, :]
        if d + 1 in POOL_WINDOWS:
            sums[d + 1] = run
    sel = sums[16]
    for w in (8, 4, 2):
        sel = jnp.where(win == w, sums[w], sel)
    pos = (j * tt + 1 + lax.broadcasted_iota(jnp.int32, (tt, 1), 0)).astype(F32)
    p = sel / jnp.minimum(pos, win.astype(F32)) - cur
    o_ref[0, :, :D_POOL] = jnp.dot(p.astype(BF16), pw_ref[...], preferred_element_type=F32) * ps_ref[...]

    acc = jnp.zeros((tt, D_CONV), F32) + cb_ref[...]
    for k in range(CONV_WIDTH):
        off = CONV_PAD - (CONV_WIDTH - 1) + k
        acc = acc + cw_ref[k:k + 1, :] * hwin_ref[off:off + tt, :]
    mu = jnp.mean(acc, axis=-1, keepdims=True)
    cen = acc - mu
    var = jnp.mean(cen * cen, axis=-1, keepdims=True)
    hn = cen * lax.rsqrt(var + EPS) * lg_ref[...] + lb_ref[...]
    o_ref[0, :, D_POOL:] = hn * jax.nn.sigmoid(hn)


def _pool_conv(up, hc, pool_w_bd, pool_scale, conv_w, conv_b, ln_g, ln_b, *, tt):
    b, s, _ = up.shape
    return pl.pallas_call(
        functools.partial(_pool_conv_kernel, tt=tt),
        out_shape=jax.ShapeDtypeStruct((b, s, D_POOL + D_CONV), F32),
        grid=(b, s // tt),
        in_specs=[
            pl.BlockSpec((1, tt, D_POOL), lambda i, j: (i, j, 0)),
            pl.BlockSpec((1, tt, D_CONV), lambda i, j: (i, j, 0)),
            _const_spec((D_POOL, D_POOL)),
            _const_spec((1, D_POOL)),
            _const_spec((CONV_WIDTH, D_CONV)),
            _const_spec((1, D_CONV)),
            _const_spec((1, D_CONV)),
            _const_spec((1, D_CONV)),
        ],
        out_specs=pl.BlockSpec((1, tt, D_POOL + D_CONV), lambda i, j: (i, j, 0)),
        scratch_shapes=[pltpu.VMEM((tt + POOL_PAD, D_POOL), F32),
                        pltpu.VMEM((tt + CONV_PAD, D_CONV), F32)],
        compiler_params=_cparams("parallel", "arbitrary"),
    )(up, hc, pool_w_bd, pool_scale, conv_w, conv_b, ln_g, ln_b)


def _mix_out_kernel(x_ref, ys_ref, us_ref, ypc_ref, d_ref, wglu_ref, wo1_ref, wo2_ref, o_ref):
    y = ys_ref[...] + d_ref[...] * us_ref[...]
    y = jax.nn.gelu(y, approximate=True)
    gate = jax.nn.sigmoid(jnp.dot(y.astype(BF16), wglu_ref[...], preferred_element_type=F32))
    acc = jnp.dot((y * gate).astype(BF16), wo1_ref[...], preferred_element_type=F32)
    acc = acc + jnp.dot(ypc_ref[...].astype(BF16), wo2_ref[...], preferred_element_type=F32)
    o_ref[...] = x_ref[...] + acc


def _mix_out(x2d, ys, us, ypc, d, w_glu, w_out1, w_out2, *, tm):
    n = x2d.shape[0]
    row = lambda w: pl.BlockSpec((tm, w), lambda i: (i, 0))
    return pl.pallas_call(
        _mix_out_kernel,
        out_shape=jax.ShapeDtypeStruct((n, D_MODEL), F32),
        grid=(n // tm,),
        in_specs=[row(D_MODEL), row(D_SSM), row(D_SSM), row(D_POOL + D_CONV),
                  _const_spec((1, D_SSM)), _const_spec((D_SSM, D_SSM)),
                  _const_spec((D_SSM, D_MODEL)), _const_spec((D_POOL + D_CONV, D_MODEL))],
        out_specs=row(D_MODEL),
        compiler_params=_cparams("parallel"),
    )(x2d, ys, us, ypc, d, w_glu, w_out1, w_out2)


def _kv_kernel(m_ref, g_ref, wk_ref, wv_ref, k_ref, v_ref):
    m = _rms(m_ref[...], g_ref[...]).astype(BF16)
    k_ref[...] = jnp.dot(m, wk_ref[...], preferred_element_type=F32).astype(BF16)
    v_ref[...] = jnp.dot(m, wv_ref[...], preferred_element_type=F32).astype(BF16)


def _kv(mem2d, norm_g, wk, wv, *, tm):
    n = mem2d.shape[0]
    row = pl.BlockSpec((tm, D_MODEL), lambda i: (i, 0))
    return pl.pallas_call(
        _kv_kernel,
        out_shape=(jax.ShapeDtypeStruct((n, D_MODEL), BF16), jax.ShapeDtypeStruct((n, D_MODEL), BF16)),
        grid=(n // tm,),
        in_specs=[row, _const_spec((1, D_MODEL)), _const_spec((D_MODEL, D_MODEL)),
                  _const_spec((D_MODEL, D_MODEL))],
        out_specs=(row, row),
        compiler_params=_cparams("parallel"),
    )(mem2d, norm_g, wk, wv)


def _xattn_kernel(x_ref, k_ref, v_ref, g_ref, wq_ref, wo_ref, o_ref):
    x = x_ref[0]
    h = _rms(x, g_ref[...]).astype(BF16)
    q = jnp.dot(h, wq_ref[...], preferred_element_type=F32) * (XHEAD_DIM ** -0.5)
    heads = []
    for hd in range(N_XHEADS):
        sl = slice(hd * XHEAD_DIM, (hd + 1) * XHEAD_DIM)
        sc = lax.dot_general(q[:, sl].astype(BF16), k_ref[0, :, sl], (((1,), (1,)), ((), ())),
                             preferred_element_type=F32)
        p = jnp.exp(sc - jnp.max(sc, axis=-1, keepdims=True))
        l = jnp.sum(p, axis=-1, keepdims=True)
        heads.append(jnp.dot(p.astype(BF16), v_ref[0, :, sl], preferred_element_type=F32) / l)
    o = jnp.concatenate(heads, axis=-1).astype(BF16)
    o_ref[0] = x + jnp.dot(o, wo_ref[...], preferred_element_type=F32)


def _xattn(x3d, k3d, v3d, norm_g, wq, wo, *, tq):
    b, s, _ = x3d.shape
    return pl.pallas_call(
        _xattn_kernel,
        out_shape=jax.ShapeDtypeStruct((b, s, D_MODEL), F32),
        grid=(b, s // tq),
        in_specs=[
            pl.BlockSpec((1, tq, D_MODEL), lambda i, j: (i, j, 0)),
            pl.BlockSpec((1, MEM_LEN, D_MODEL), lambda i, j: (i, 0, 0)),
            pl.BlockSpec((1, MEM_LEN, D_MODEL), lambda i, j: (i, 0, 0)),
            _const_spec((1, D_MODEL)),
            _const_spec((D_MODEL, D_MODEL)),
            _const_spec((D_MODEL, D_MODEL)),
        ],
        out_specs=pl.BlockSpec((1, tq, D_MODEL), lambda i, j: (i, j, 0)),
        compiler_params=_cparams("parallel", "parallel"),
    )(x3d, k3d, v3d, norm_g, wq, wo)


def _tile(n, pref):
    t = min(n, pref)
    assert n % t == 0, (n, t)
    return t


def kernel(x, mem, ffn1_norm, ffn1_w_gate, ffn1_w_up, ffn1_w_down, mix_norm, w_in, w_out, ssm_lambda_re, ssm_lambda_im, ssm_log_dt, ssm_b_re, ssm_b_im, ssm_c_re, ssm_c_im, ssm_d, ssm_w_glu, pool_w, pool_scale, conv_w, conv_b, conv_ln_g, conv_ln_b, xattn_norm, mem_norm, xattn_wq, xattn_wk, xattn_wv, xattn_wo, ffn2_norm, ffn2_w_gate, ffn2_w_up, ffn2_w_down, final_norm):
    bsz, seq, _ = x.shape
    depth = w_in.shape[0]
    n = bsz * seq
    assert seq % S5_CHUNK == 0
    n_chunks = seq // S5_CHUNK
    assert n_chunks & (n_chunks - 1) == 0 and 2 * n_chunks.bit_length() <= SSM_GROUP
    tm = _tile(n, 512)
    tq = _tile(seq, 512)
    tt = _tile(seq, 256)
    tkv = _tile(bsz * MEM_LEN, 512)
    bf = lambda w: w.astype(BF16)
    row = lambda v: v.reshape(1, -1)

    x2d = x.reshape(n, D_MODEL)
    mem2d = mem.reshape(bsz * MEM_LEN, D_MODEL)
    for l in range(depth):
        x2d = _ffn(x2d, row(ffn1_norm[l]), bf(ffn1_w_gate[l]), bf(ffn1_w_up[l]), bf(ffn1_w_down[l]),
                   row(final_norm), final=False, tm=tm)

        us, up, hc = _mix_in(x2d, row(mix_norm[l]), bf(w_in[l]), tm=tm)
        zt, wst, gt, cp = _s5_prep(ssm_lambda_re[l], ssm_lambda_im[l], ssm_log_dt[l], ssm_b_re[l],
                                   ssm_b_im[l], ssm_c_re[l], ssm_c_im[l], n_chunks=n_chunks)
        ut = us.reshape(bsz * n_chunks, S5_CHUNK, N_SSM_GROUPS, SSM_GROUP).transpose(2, 0, 1, 3)
        ut = ut.reshape(N_SSM_GROUPS, bsz * n_chunks, S5_ROWS).astype(BF16)
        yt = _s5_main(ut, zt, wst, gt, cp, n_chunks=n_chunks)
        ys = yt.reshape(N_SSM_GROUPS, bsz * n_chunks, S5_CHUNK, SSM_GROUP).transpose(1, 2, 0, 3)
        ys = ys.reshape(n, D_SSM)
        pool_bd = jax.scipy.linalg.block_diag(*[pool_w[l, g] for g in range(len(POOL_WINDOWS))])
        ypc = _pool_conv(up.reshape(bsz, seq, D_POOL), hc.reshape(bsz, seq, D_CONV), bf(pool_bd),
                         row(pool_scale[l]), conv_w[l], row(conv_b[l]), row(conv_ln_g[l]),
                         row(conv_ln_b[l]), tt=tt)
        x2d = _mix_out(x2d, ys, us, ypc.reshape(n, D_POOL + D_CONV), row(ssm_d[l]), bf(ssm_w_glu[l]),
                       bf(w_out[l, :D_SSM]), bf(w_out[l, D_SSM:]), tm=tm)

        k2d, v2d = _kv(mem2d, row(mem_norm[l]), bf(xattn_wk[l]), bf(xattn_wv[l]), tm=tkv)
        x3d = _xattn(x2d.reshape(bsz, seq, D_MODEL), k2d.reshape(bsz, MEM_LEN, D_MODEL),
                     v2d.reshape(bsz, MEM_LEN, D_MODEL), row(xattn_norm[l]), bf(xattn_wq[l]),
                     bf(xattn_wo[l]), tq=tq)
        x2d = x3d.reshape(n, D_MODEL)

        x2d = _ffn(x2d, row(ffn2_norm[l]), bf(ffn2_w_gate[l]), bf(ffn2_w_up[l]), bf(ffn2_w_down[l]),
                   row(final_norm), final=(l == depth - 1), tm=tm)
    return x2d.reshape(bsz, seq, D_MODEL)
```

```python
import functools

import jax
import jax.numpy as jnp
from jax import lax
from jax.experimental import pallas as pl
from jax.experimental.pallas import tpu as pltpu

F32 = jnp.float32
BF16 = jnp.bfloat16

D_MODEL = 1024
MEM_LEN = 256
D_SSM = 384
D_POOL = 256
D_CONV = 384
D_PC = D_POOL + D_CONV
SSM_GROUP = 16
N_SSM_GROUPS = 24
SSM_STATE = 64
POOL_WINDOWS = (2, 4, 8, 16)
POOL_GROUP = 64
CONV_WIDTH = 31
D_IN = D_SSM + D_POOL + 2 * D_CONV
D_FF = 2816
N_XHEADS = 4
XHEAD_DIM = 256
EPS = 1e-6

SUBLANES = 8
LANES = 128
MXU_TILE = 256

FF_CHUNK = MXU_TILE
S5_CHUNK = 64
S5_SUB = MXU_TILE // SSM_GROUP
S5_NB = S5_CHUNK // S5_SUB
S5_ROWS = S5_CHUNK * SSM_GROUP
S5_MAX_SCAN_STEPS = 6
CONV_PAD = 32
POOL_PAD = 16
VMEM_LIMIT = 56 * 1024 * 1024


def _cparams(*sem):
    return pltpu.CompilerParams(dimension_semantics=sem, vmem_limit_bytes=VMEM_LIMIT)


def _const_spec(shape):
    nd = len(shape)
    return pl.BlockSpec(shape, lambda *_: (0,) * nd, pipeline_mode=pl.Buffered(1))


def _rms(x, g):
    ms = jnp.mean(x * x, axis=-1, keepdims=True)
    return x * lax.rsqrt(ms + EPS) * g


def _ffn_body(x, g_ref, wg_ref, wu_ref, wd_ref):
    h = _rms(x, g_ref[...]).astype(BF16)
    acc = jnp.zeros(x.shape, F32)
    for c in range(D_FF // FF_CHUNK):
        sl = slice(c * FF_CHUNK, (c + 1) * FF_CHUNK)
        gate = jnp.dot(h, wg_ref[:, sl], preferred_element_type=F32)
        up = jnp.dot(h, wu_ref[:, sl], preferred_element_type=F32)
        act = (gate * jax.nn.sigmoid(gate) * up).astype(BF16)
        acc = acc + jnp.dot(act, wd_ref[sl, :], preferred_element_type=F32)
    return x + 0.5 * acc


def _ffn_mix_in_kernel(x_ref, g1_ref, wg_ref, wu_ref, wd_ref, g2_ref, wst_ref, wrest_ref,
                       xo_ref, u4_ref, up_ref, hc_ref):
    b, nc, _ = x_ref.shape
    y = _ffn_body(x_ref[...].reshape(b * nc, D_MODEL), g1_ref, wg_ref, wu_ref, wd_ref)
    xo_ref[...] = y.reshape(b, nc, D_MODEL)
    h = _rms(y, g2_ref[...]).astype(BF16)
    ut = lax.dot_general(wst_ref[...], h, (((1,), (1,)), ((), ())), preferred_element_type=F32)
    u4_ref[...] = ut.reshape(N_SSM_GROUPS, SSM_GROUP, b * nc)
    z = jnp.dot(h, wrest_ref[...], preferred_element_type=F32)
    up_ref[...] = z[:, :D_POOL].reshape(b, nc, D_POOL)
    v = z[:, D_POOL:D_POOL + D_CONV]
    g = z[:, D_POOL + D_CONV:]
    hc_ref[...] = (v * jax.nn.sigmoid(g)).reshape(b, nc, D_CONV)


def _step_spec(b, nc, width):
    return pl.BlockSpec((b, None, nc, width), lambda i: (0, i, 0, 0))


def _chan_spec(ncols):
    return pl.BlockSpec((N_SSM_GROUPS, None, SSM_GROUP, ncols), lambda i: (0, i, 0, 0))


def _ffn_mix_in(x4, g1, wg, wu, wd, g2, w_ssm_t, w_rest):
    b, t, nc, _ = x4.shape
    ncols = b * nc
    return pl.pallas_call(
        _ffn_mix_in_kernel,
        out_shape=(jax.ShapeDtypeStruct((b, t, nc, D_MODEL), F32),
                   jax.ShapeDtypeStruct((N_SSM_GROUPS, t, SSM_GROUP, ncols), F32),
                   jax.ShapeDtypeStruct((b, t, nc, D_POOL), F32),
                   jax.ShapeDtypeStruct((b, t, nc, D_CONV), F32)),
        grid=(t,),
        in_specs=[
            _step_spec(b, nc, D_MODEL),
            _const_spec((1, D_MODEL)),
            _const_spec((D_MODEL, D_FF)),
            _const_spec((D_MODEL, D_FF)),
            _const_spec((D_FF, D_MODEL)),
            _const_spec((1, D_MODEL)),
            _const_spec((D_SSM, D_MODEL)),
            _const_spec((D_MODEL, D_IN - D_SSM)),
        ],
        out_specs=(_step_spec(b, nc, D_MODEL), _chan_spec(ncols),
                   _step_spec(b, nc, D_POOL), _step_spec(b, nc, D_CONV)),
        compiler_params=_cparams("parallel"),
    )(x4, g1, wg, wu, wd, g2, w_ssm_t, w_rest)


def _s5_prep_kernel(lr2_ref, li2_ref, ldt_ref, br2_ref, bi2_ref, lrc_ref, lic_ref, crt_ref, cit_ref,
                    zt_ref, wst_ref, gt_ref, cp_ref):
    hi = lax.Precision.HIGHEST
    lane = lax.broadcasted_iota(jnp.int32, (1, LANES), 1)
    first = lane < SSM_STATE

    lr2 = lr2_ref[0]
    li2 = li2_ref[0]
    dt = jnp.exp(ldt_ref[0])
    rho = lr2 * dt
    th = li2 * dt

    def pw_row(e):
        mag = jnp.exp(e * rho)
        return mag * jnp.cos(e * th), mag * jnp.sin(e * th)

    one = jnp.ones((1, 1), F32)
    a_r, a_i = pw_row(one)
    den = lr2 * lr2 + li2 * li2
    z_r = ((a_r - 1.0) * lr2 + a_i * li2) / den
    z_i = (a_i * lr2 - (a_r - 1.0) * li2) / den
    br2 = br2_ref[0]
    bi2 = bi2_ref[0]
    b1 = jnp.where(first, br2, bi2)
    b2 = jnp.where(first, -bi2, br2)
    y1 = z_r * b1 + z_i * b2
    y2 = z_r * b2 - z_i * b1

    e_lo = (S5_SUB - 1 - lax.broadcasted_iota(jnp.int32, (S5_SUB, 1), 0)).astype(F32)
    lo_r, lo_i = pw_row(e_lo)
    lob1 = jnp.concatenate([lo_r[i:i + 1] * y1 + lo_i[i:i + 1] * y2 for i in range(S5_SUB)], axis=0)
    lob2 = jnp.concatenate([lo_r[i:i + 1] * y2 - lo_i[i:i + 1] * y1 for i in range(S5_SUB)], axis=0)
    e_hi = (S5_SUB * (S5_NB - 1 - lax.broadcasted_iota(jnp.int32, (S5_NB, 1), 0))).astype(F32)
    hi_r, hi_i = pw_row(e_hi)
    pb = jnp.concatenate([hi_r[m:m + 1] * lob1 + hi_i[m:m + 1] * lob2 for m in range(S5_NB)], axis=0)
    wst_ref[0] = pb.astype(BF16)

    lrc = lrc_ref[0]
    lic = lic_ref[0]
    rho_c = lrc * dt
    th_c = lic * dt
    mag_c = jnp.exp(rho_c)
    ac_r = mag_c * jnp.cos(th_c)
    ac_i = mag_c * jnp.sin(th_c)
    rc_r, rc_i = [], []
    for half in range(MXU_TILE // LANES):
        e = (lane // SSM_GROUP + half * (LANES // SSM_GROUP)).astype(F32)
        mag = jnp.exp(e * rho_c)
        t_r = mag * jnp.cos(e * th_c)
        t_i = mag * jnp.sin(e * th_c)
        c_r = crt_ref[0, :, half * LANES:(half + 1) * LANES]
        c_i = cit_ref[0, :, half * LANES:(half + 1) * LANES]
        rc_r.append(t_r * c_r - t_i * c_i)
        rc_i.append(t_r * c_i + t_i * c_r)
    rc_stack = jnp.concatenate([jnp.concatenate(rc_r, axis=1), -jnp.concatenate(rc_i, axis=1)], axis=0)
    ct_stack = jnp.concatenate([crt_ref[0], -cit_ref[0]], axis=0)

    def csq(v_r, v_i):
        return v_r * v_r - v_i * v_i, 2.0 * v_r * v_i

    a16_r, a16_i = ac_r, ac_i
    for _ in range(S5_SUB.bit_length() - 1):
        a16_r, a16_i = csq(a16_r, a16_i)
    g_r, g_i = ac_r, ac_i
    for j in range(S5_NB):
        for half in range(MXU_TILE // LANES):
            lo = j * MXU_TILE + half * LANES
            gt_ref[0, :SSM_STATE, lo:lo + LANES] = (g_r * rc_r[half] - g_i * rc_i[half]).astype(BF16)
            gt_ref[0, SSM_STATE:, lo:lo + LANES] = (-(g_r * rc_i[half] + g_i * rc_r[half])).astype(BF16)
        g_r, g_i = g_r * a16_r - g_i * a16_i, g_r * a16_i + g_i * a16_r

    w_r, w_i = a16_r, a16_i
    for _ in range(S5_NB.bit_length() - 1):
        w_r, w_i = csq(w_r, w_i)
    for r in range(S5_MAX_SCAN_STEPS):
        cp_ref[0, 2 * r * SSM_STATE:(2 * r + 1) * SSM_STATE, :] = w_r
        cp_ref[0, (2 * r + 1) * SSM_STATE:(2 * r + 2) * SSM_STATE, :] = w_i
        w_r, w_i = csq(w_r, w_i)

    off_rows = (S5_CHUNK - S5_SUB) * SSM_GROUP
    lo_row = (S5_SUB - 1) * SSM_GROUP
    zt_off = jnp.dot(pb[lo_row:lo_row + off_rows], rc_stack, precision=hi, preferred_element_type=F32)
    zt_ref[0, :off_rows, :] = zt_off.astype(BF16)
    kall = jnp.dot(pb[off_rows:], ct_stack, precision=hi, preferred_element_type=F32)
    colblk = lax.broadcasted_iota(jnp.int32, (1, MXU_TILE), 1) // SSM_GROUP
    zd = jnp.zeros((MXU_TILE, MXU_TILE), F32)
    for jj in range(S5_SUB):
        sh = (S5_SUB - 1 - jj) * SSM_GROUP
        shifted = kall if sh == 0 else jnp.concatenate(
            [kall[sh:], jnp.zeros((sh, MXU_TILE), F32)], axis=0)
        zd = jnp.where(colblk == jj, shifted, zd)
    zt_ref[0, off_rows:, :] = zd.astype(BF16)


def _s5_prep(lam_re, lam_im, log_dt, b_re, b_im, c_re, c_im):
    g = N_SSM_GROUPS
    lr2 = jnp.tile(lam_re, (1, 2))[:, None, :]
    li2 = jnp.tile(lam_im, (1, 2))[:, None, :]
    ldt = jnp.broadcast_to(log_dt[:, None, None], (g, 1, LANES))
    br2 = jnp.tile(jnp.swapaxes(b_re, 1, 2), (1, 1, 2))
    bi2 = jnp.tile(jnp.swapaxes(b_im, 1, 2), (1, 1, 2))
    lrc = jnp.broadcast_to(lam_re[:, :, None], (g, SSM_STATE, LANES))
    lic = jnp.broadcast_to(lam_im[:, :, None], (g, SSM_STATE, LANES))
    crt = jnp.tile(jnp.swapaxes(c_re, 1, 2), (1, 1, S5_SUB))
    cit = jnp.tile(jnp.swapaxes(c_im, 1, 2), (1, 1, S5_SUB))

    def spec(*shape):
        return pl.BlockSpec((1,) + shape, lambda i: (i, 0, 0))

    cp_rows = 2 * S5_MAX_SCAN_STEPS * SSM_STATE
    return pl.pallas_call(
        _s5_prep_kernel,
        out_shape=(jax.ShapeDtypeStruct((g, S5_ROWS, MXU_TILE), BF16),
                   jax.ShapeDtypeStruct((g, S5_ROWS, LANES), BF16),
                   jax.ShapeDtypeStruct((g, LANES, S5_ROWS), BF16),
                   jax.ShapeDtypeStruct((g, cp_rows, LANES), F32)),
        grid=(g,),
        in_specs=[spec(1, LANES), spec(1, LANES), spec(1, LANES),
                  spec(SSM_GROUP, LANES), spec(SSM_GROUP, LANES),
                  spec(SSM_STATE, LANES), spec(SSM_STATE, LANES),
                  spec(SSM_STATE, MXU_TILE), spec(SSM_STATE, MXU_TILE)],
        out_specs=(spec(S5_ROWS, MXU_TILE), spec(S5_ROWS, LANES), spec(LANES, S5_ROWS),
                   spec(cp_rows, LANES)),
        compiler_params=_cparams("parallel"),
    )(lr2, li2, ldt, br2, bi2, lrc, lic, crt, cit)


def _s5_main_kernel(u_ref, zw_ref, wst_ref, g_ref, cp_ref, o_ref, *, n_chunks):
    ncols = u_ref.shape[-1]
    u = u_ref[0].reshape(S5_ROWS, ncols).astype(BF16)
    state = jnp.dot(wst_ref[0], u, preferred_element_type=F32)
    s_r, s_i = state[:SSM_STATE], state[SSM_STATE:]
    chunk = lax.broadcasted_iota(jnp.int32, (1, ncols), 1) % n_chunks
    reps = ncols // LANES if ncols > LANES else 1

    def wide(v):
        return jnp.concatenate([v] * reps, axis=1)[:, :ncols]

    r = 0
    while (1 << r) < n_chunks:
        s = 1 << r
        p_r = jnp.where(chunk >= s, pltpu.roll(s_r, s, 1), 0.0)
        p_i = jnp.where(chunk >= s, pltpu.roll(s_i, s, 1), 0.0)
        w_r = wide(cp_ref[0, 2 * r * SSM_STATE:(2 * r + 1) * SSM_STATE, :])
        w_i = wide(cp_ref[0, (2 * r + 1) * SSM_STATE:(2 * r + 2) * SSM_STATE, :])
        s_r, s_i = s_r + w_r * p_r - w_i * p_i, s_i + w_r * p_i + w_i * p_r
        r += 1
    c_r = jnp.where(chunk >= 1, pltpu.roll(s_r, 1, 1), 0.0)
    c_i = jnp.where(chunk >= 1, pltpu.roll(s_i, 1, 1), 0.0)
    carry = jnp.concatenate([c_r, c_i], axis=0).astype(BF16)
    for j in range(S5_NB):
        k = (j + 1) * MXU_TILE
        y = jnp.dot(zw_ref[0, :, (S5_NB - 1 - j) * MXU_TILE:], u[:k], preferred_element_type=F32)
        y = y + jnp.dot(g_ref[0, j * MXU_TILE:(j + 1) * MXU_TILE, :], carry, preferred_element_type=F32)
        o_ref[0, j * S5_SUB:(j + 1) * S5_SUB] = y.reshape(S5_SUB, SSM_GROUP, ncols)


def _s5_main(u4, zw, wst, gm, cp, *, n_chunks):
    g, t, _, ncols = u4.shape
    cp_rows = cp.shape[1]

    def spec(*shape):
        return pl.BlockSpec((1,) + shape, lambda i: (i,) + (0,) * len(shape))

    return pl.pallas_call(
        functools.partial(_s5_main_kernel, n_chunks=n_chunks),
        out_shape=jax.ShapeDtypeStruct((g, t, SSM_GROUP, ncols), F32),
        grid=(g,),
        in_specs=[spec(t, SSM_GROUP, ncols), spec(MXU_TILE, S5_ROWS), spec(LANES, S5_ROWS),
                  spec(S5_ROWS, LANES), spec(cp_rows, LANES)],
        out_specs=spec(t, SSM_GROUP, ncols),
        compiler_params=_cparams("parallel"),
    )(u4, zw, wst, gm, cp)


def _pool_conv_kernel(up_ref, hc_ref, pw_ref, ps_ref, cw_ref, cb_ref, lg_ref, lb_ref, o_ref,
                      uext_ref, hext_ref, p_ref, c_ref):
    t, nc = up_ref.shape[1], up_ref.shape[2]
    chunk = lax.broadcasted_iota(jnp.int32, (nc, 1), 0)

    def from_prev_chunk(v):
        return jnp.where(chunk >= 1, pltpu.roll(v, 1, 0), 0.0)

    uext_ref[POOL_PAD:] = up_ref[0]
    hext_ref[CONV_PAD:] = hc_ref[0]
    for m in range(1, POOL_PAD + 1):
        uext_ref[POOL_PAD - m] = from_prev_chunk(up_ref[0, t - m])
    for m in range(1, CONV_PAD + 1):
        hext_ref[CONV_PAD - m] = from_prev_chunk(hc_ref[0, t - m])

    lane = lax.broadcasted_iota(jnp.int32, (1, D_POOL), 1)
    win = jnp.left_shift(2, lane // POOL_GROUP)
    win_f = win.astype(F32)
    first_tap = CONV_PAD - (CONV_WIDTH - 1)

    def step(i, carry):
        cur = uext_ref[POOL_PAD + i]
        run = cur
        sums = {}
        for d in range(1, POOL_WINDOWS[-1]):
            run = run + uext_ref[POOL_PAD + i - d]
            if d + 1 in POOL_WINDOWS:
                sums[d + 1] = run
        sel = sums[16]
        for w in (8, 4, 2):
            sel = jnp.where(win == w, sums[w], sel)
        pos = (chunk * t + i + 1).astype(F32)
        p_ref[i] = sel / jnp.minimum(pos, win_f) - cur

        acc = jnp.zeros((nc, D_CONV), F32) + cb_ref[...]
        for k in range(CONV_WIDTH):
            acc = acc + cw_ref[k:k + 1, :] * hext_ref[i + first_tap + k]
        c_ref[i] = acc
        return carry

    lax.fori_loop(0, t, step, 0)
    pm = jnp.dot(p_ref[...].reshape(t * nc, D_POOL).astype(BF16), pw_ref[...], preferred_element_type=F32)
    o_ref[0, :, :, :D_POOL] = (pm * ps_ref[...]).reshape(t, nc, D_POOL)
    conv = c_ref[...].reshape(t * nc, D_CONV)
    mu = jnp.mean(conv, axis=-1, keepdims=True)
    cen = conv - mu
    var = jnp.mean(cen * cen, axis=-1, keepdims=True)
    hn = cen * lax.rsqrt(var + EPS) * lg_ref[...] + lb_ref[...]
    o_ref[0, :, :, D_POOL:] = (hn * jax.nn.sigmoid(hn)).reshape(t, nc, D_CONV)


def _pool_conv(up4, hc4, pool_w_bd, pool_scale, conv_w, conv_b, ln_g, ln_b):
    b, t, nc, _ = up4.shape
    assert t >= CONV_PAD and t >= POOL_PAD
    blk = lambda w: pl.BlockSpec((1, t, nc, w), lambda i: (i, 0, 0, 0))
    return pl.pallas_call(
        _pool_conv_kernel,
        out_shape=jax.ShapeDtypeStruct((b, t, nc, D_PC), F32),
        grid=(b,),
        in_specs=[
            blk(D_POOL), blk(D_CONV),
            _const_spec((D_POOL, D_POOL)),
            _const_spec((1, D_POOL)),
            _const_spec((CONV_WIDTH, D_CONV)),
            _const_spec((1, D_CONV)),
            _const_spec((1, D_CONV)),
            _const_spec((1, D_CONV)),
        ],
        out_specs=blk(D_PC),
        scratch_shapes=[pltpu.VMEM((t + POOL_PAD, nc, D_POOL), F32),
                        pltpu.VMEM((t + CONV_PAD, nc, D_CONV), F32),
                        pltpu.VMEM((t, nc, D_POOL), F32),
                        pltpu.VMEM((t, nc, D_CONV), F32)],
        compiler_params=_cparams("parallel"),
    )(up4, hc4, pool_w_bd, pool_scale, conv_w, conv_b, ln_g, ln_b)


def _mix_out_kernel(x_ref, y4_ref, u4_ref, ypc_ref, d_ref, wglut_ref, wo1_ref, wo2_ref, o_ref):
    b, nc, _ = x_ref.shape
    ncols = b * nc
    yt = y4_ref[...].reshape(D_SSM, ncols) + d_ref[...] * u4_ref[...].reshape(D_SSM, ncols)
    yt = jax.nn.gelu(yt, approximate=True)
    gate = jax.nn.sigmoid(jnp.dot(wglut_ref[...], yt.astype(BF16), preferred_element_type=F32))
    ys = (yt * gate).T.astype(BF16)
    acc = jnp.dot(ys, wo1_ref[...], preferred_element_type=F32)
    acc = acc + jnp.dot(ypc_ref[...].reshape(ncols, D_PC).astype(BF16), wo2_ref[...],
                        preferred_element_type=F32)
    o_ref[...] = x_ref[...] + acc.reshape(b, nc, D_MODEL)


def _mix_out(x4, y4, u4, ypc4, d_col, w_glu_t, w_out1, w_out2):
    b, t, nc, _ = x4.shape
    ncols = b * nc
    return pl.pallas_call(
        _mix_out_kernel,
        out_shape=jax.ShapeDtypeStruct((b, t, nc, D_MODEL), F32),
        grid=(t,),
        in_specs=[_step_spec(b, nc, D_MODEL), _chan_spec(ncols), _chan_spec(ncols),
                  _step_spec(b, nc, D_PC),
                  _const_spec((D_SSM, 1)), _const_spec((D_SSM, D_SSM)),
                  _const_spec((D_SSM, D_MODEL)), _const_spec((D_PC, D_MODEL))],
        out_specs=_step_spec(b, nc, D_MODEL),
        compiler_params=_cparams("parallel"),
    )(x4, y4, u4, ypc4, d_col, w_glu_t, w_out1, w_out2)


def _kv_kernel(m_ref, g_ref, wk_ref, wv_ref, k_ref, v_ref):
    m = _rms(m_ref[...], g_ref[...]).astype(BF16)
    k_ref[...] = jnp.dot(m, wk_ref[...], preferred_element_type=F32).astype(BF16)
    v_ref[...] = jnp.dot(m, wv_ref[...], preferred_element_type=F32).astype(BF16)


def _kv(mem2d, norm_g, wk, wv, *, tm):
    n = mem2d.shape[0]
    row = pl.BlockSpec((tm, D_MODEL), lambda i: (i, 0))
    return pl.pallas_call(
        _kv_kernel,
        out_shape=(jax.ShapeDtypeStruct((n, D_MODEL), BF16), jax.ShapeDtypeStruct((n, D_MODEL), BF16)),
        grid=(n // tm,),
        in_specs=[row, _const_spec((1, D_MODEL)), _const_spec((D_MODEL, D_MODEL)),
                  _const_spec((D_MODEL, D_MODEL))],
        out_specs=(row, row),
        compiler_params=_cparams("parallel"),
    )(mem2d, norm_g, wk, wv)


def _xattn_ffn_kernel(x_ref, k_ref, v_ref, ga_ref, wq_ref, wo_ref, gf_ref, wg_ref, wu_ref, wd_ref,
                      fg_ref, o_ref, *, final):
    _, ts, nc, _ = x_ref.shape
    x = x_ref[0].reshape(ts * nc, D_MODEL)
    h = _rms(x, ga_ref[...]).astype(BF16)
    q = jnp.dot(h, wq_ref[...], preferred_element_type=F32) * (XHEAD_DIM ** -0.5)
    heads = []
    for hd in range(N_XHEADS):
        sl = slice(hd * XHEAD_DIM, (hd + 1) * XHEAD_DIM)
        sc = lax.dot_general(q[:, sl].astype(BF16), k_ref[0, :, sl], (((1,), (1,)), ((), ())),
                             preferred_element_type=F32)
        p = jnp.exp(sc - jnp.max(sc, axis=-1, keepdims=True))
        l = jnp.sum(p, axis=-1, keepdims=True)
        heads.append(jnp.dot(p.astype(BF16), v_ref[0, :, sl], preferred_element_type=F32) / l)
    o = jnp.concatenate(heads, axis=-1).astype(BF16)
    x = x + jnp.dot(o, wo_ref[...], preferred_element_type=F32)
    y = _ffn_body(x, gf_ref, wg_ref, wu_ref, wd_ref)
    if final:
        y = _rms(y, fg_ref[...])
    o_ref[0] = y.reshape(ts, nc, D_MODEL)


def _xattn_ffn(x4, k3d, v3d, ga, wq, wo, gf, wg, wu, wd, fg, *, final, ts):
    b, t, nc, _ = x4.shape
    xspec = pl.BlockSpec((1, ts, nc, D_MODEL), lambda i, j: (i, j, 0, 0))
    return pl.pallas_call(
        functools.partial(_xattn_ffn_kernel, final=final),
        out_shape=jax.ShapeDtypeStruct((b, t, nc, D_MODEL), F32),
        grid=(b, t // ts),
        in_specs=[
            xspec,
            pl.BlockSpec((1, MEM_LEN, D_MODEL), lambda i, j: (i, 0, 0)),
            pl.BlockSpec((1, MEM_LEN, D_MODEL), lambda i, j: (i, 0, 0)),
            _const_spec((1, D_MODEL)),
            _const_spec((D_MODEL, D_MODEL)),
            _const_spec((D_MODEL, D_MODEL)),
            _const_spec((1, D_MODEL)),
            _const_spec((D_MODEL, D_FF)),
            _const_spec((D_MODEL, D_FF)),
            _const_spec((D_FF, D_MODEL)),
            _const_spec((1, D_MODEL)),
        ],
        out_specs=xspec,
        compiler_params=_cparams("parallel", "parallel"),
    )(x4, k3d, v3d, ga, wq, wo, gf, wg, wu, wd, fg)


def _tile(n, pref):
    t = min(n, pref)
    assert n % t == 0, (n, t)
    return t


def kernel(x, mem, ffn1_norm, ffn1_w_gate, ffn1_w_up, ffn1_w_down, mix_norm, w_in, w_out, ssm_lambda_re, ssm_lambda_im, ssm_log_dt, ssm_b_re, ssm_b_im, ssm_c_re, ssm_c_im, ssm_d, ssm_w_glu, pool_w, pool_scale, conv_w, conv_b, conv_ln_g, conv_ln_b, xattn_norm, mem_norm, xattn_wq, xattn_wk, xattn_wv, xattn_wo, ffn2_norm, ffn2_w_gate, ffn2_w_up, ffn2_w_down, final_norm):
    bsz, seq, _ = x.shape
    depth = w_in.shape[0]
    assert seq % S5_CHUNK == 0
    n_chunks = seq // S5_CHUNK
    assert n_chunks & (n_chunks - 1) == 0 and n_chunks <= (1 << S5_MAX_SCAN_STEPS)
    ts = _tile(S5_CHUNK, max(1, 512 // n_chunks))
    tkv = _tile(bsz * MEM_LEN, 512)
    bf = lambda w: w.astype(BF16)
    row = lambda v: v.reshape(1, -1)

    x4 = x.reshape(bsz, n_chunks, S5_CHUNK, D_MODEL).swapaxes(1, 2)
    mem2d = mem.reshape(bsz * MEM_LEN, D_MODEL)
    for l in range(depth):
        x4, u4, up4, hc4 = _ffn_mix_in(
            x4, row(ffn1_norm[l]), bf(ffn1_w_gate[l]), bf(ffn1_w_up[l]), bf(ffn1_w_down[l]),
            row(mix_norm[l]), bf(w_in[l, :, :D_SSM].T), bf(w_in[l, :, D_SSM:]))

        zt, wst_t, gt, cp = _s5_prep(ssm_lambda_re[l], ssm_lambda_im[l], ssm_log_dt[l], ssm_b_re[l],
                                     ssm_b_im[l], ssm_c_re[l], ssm_c_im[l])
        y4 = _s5_main(u4, jnp.swapaxes(zt, 1, 2), jnp.swapaxes(wst_t, 1, 2), jnp.swapaxes(gt, 1, 2), cp,
                      n_chunks=n_chunks)

        pool_bd = jax.scipy.linalg.block_diag(*[pool_w[l, g] for g in range(len(POOL_WINDOWS))])
        ypc4 = _pool_conv(up4, hc4, bf(pool_bd), row(pool_scale[l]), conv_w[l], row(conv_b[l]),
                          row(conv_ln_g[l]), row(conv_ln_b[l]))

        x4 = _mix_out(x4, y4, u4, ypc4, ssm_d[l].reshape(D_SSM, 1),
                      bf(ssm_w_glu[l].T), bf(w_out[l, :D_SSM]), bf(w_out[l, D_SSM:]))

        k2d, v2d = _kv(mem2d, row(mem_norm[l]), bf(xattn_wk[l]), bf(xattn_wv[l]), tm=tkv)
        x4 = _xattn_ffn(x4, k2d.reshape(bsz, MEM_LEN, D_MODEL), v2d.reshape(bsz, MEM_LEN, D_MODEL),
                        row(xattn_norm[l]), bf(xattn_wq[l]), bf(xattn_wo[l]), row(ffn2_norm[l]),
                        bf(ffn2_w_gate[l]), bf(ffn2_w_up[l]), bf(ffn2_w_down[l]), row(final_norm),
                        final=(l == depth - 1), ts=ts)
    return x4.swapaxes(1, 2).reshape(bsz, seq, D_MODEL)
```

```python
import functools

import jax
import jax.numpy as jnp
from jax import lax
from jax.experimental import pallas as pl
from jax.experimental.pallas import tpu as pltpu

F32 = jnp.float32
BF16 = jnp.bfloat16

D_MODEL = 1024
MEM_LEN = 256
D_SSM = 384
D_POOL = 256
D_CONV = 384
D_PC = D_POOL + D_CONV
SSM_GROUP = 16
N_SSM_GROUPS = 24
SSM_STATE = 64
POOL_WINDOWS = (2, 4, 8, 16)
POOL_GROUP = 64
CONV_WIDTH = 31
D_IN = D_SSM + D_POOL + 2 * D_CONV
D_FF = 2816
N_XHEADS = 4
XHEAD_DIM = 256
EPS = 1e-6

LANES = 128
MXU_TILE = 256

TOKEN_TILE = 512
FF_CHUNK = MXU_TILE
S5_SUB = MXU_TILE // SSM_GROUP
S5_CHUNK = S5_SUB
S5_NB = S5_CHUNK // S5_SUB
S5_ROWS = S5_CHUNK * SSM_GROUP
S5_MAX_SCAN_STEPS = 7
CONV_PAD = 32
POOL_PAD = 16
VMEM_LIMIT = 56 * 1024 * 1024


def _cparams(*sem):
    return pltpu.CompilerParams(dimension_semantics=sem, vmem_limit_bytes=VMEM_LIMIT)


def _const_spec(shape):
    nd = len(shape)
    return pl.BlockSpec(shape, lambda *_: (0,) * nd, pipeline_mode=pl.Buffered(1))


def _rms(x, g):
    ms = jnp.mean(x * x, axis=-1, keepdims=True)
    return x * lax.rsqrt(ms + EPS) * g


def _ffn_body(x, g_ref, wg_ref, wu_ref, wd_ref):
    h = _rms(x, g_ref[...]).astype(BF16)
    acc = jnp.zeros(x.shape, F32)
    for c in range(D_FF // FF_CHUNK):
        sl = slice(c * FF_CHUNK, (c + 1) * FF_CHUNK)
        gate = jnp.dot(h, wg_ref[:, sl], preferred_element_type=F32)
        up = jnp.dot(h, wu_ref[:, sl], preferred_element_type=F32)
        act = (gate * jax.nn.sigmoid(gate) * up).astype(BF16)
        acc = acc + jnp.dot(act, wd_ref[sl, :], preferred_element_type=F32)
    return x + 0.5 * acc


def _ffn_mix_in_kernel(x_ref, g1_ref, wg_ref, wu_ref, wd_ref, g2_ref, wst_ref, wrest_ref,
                       xo_ref, u4_ref, up_ref, hc_ref):
    b, nc, _ = x_ref.shape
    y = _ffn_body(x_ref[...].reshape(b * nc, D_MODEL), g1_ref, wg_ref, wu_ref, wd_ref)
    xo_ref[...] = y.reshape(b, nc, D_MODEL)
    h = _rms(y, g2_ref[...]).astype(BF16)
    ut = lax.dot_general(wst_ref[...], h, (((1,), (1,)), ((), ())), preferred_element_type=F32)
    u4_ref[...] = ut.reshape(N_SSM_GROUPS, SSM_GROUP, b * nc)
    z = jnp.dot(h, wrest_ref[...], preferred_element_type=F32)
    up_ref[...] = z[:, :D_POOL].reshape(b, nc, D_POOL)
    v = z[:, D_POOL:D_POOL + D_CONV]
    g = z[:, D_POOL + D_CONV:]
    hc_ref[...] = (v * jax.nn.sigmoid(g)).reshape(b, nc, D_CONV)


def _ffn_mix_in(x4, g1, wg, wu, wd, g2, w_ssm_t, w_rest, *, bg):
    b, t, nc, _ = x4.shape
    step = lambda w: pl.BlockSpec((bg, None, nc, w), lambda g, i: (g, i, 0, 0))
    return pl.pallas_call(
        _ffn_mix_in_kernel,
        out_shape=(jax.ShapeDtypeStruct((b, t, nc, D_MODEL), F32),
                   jax.ShapeDtypeStruct((N_SSM_GROUPS, t, SSM_GROUP, b * nc), F32),
                   jax.ShapeDtypeStruct((b, t, nc, D_POOL), F32),
                   jax.ShapeDtypeStruct((b, t, nc, D_CONV), F32)),
        grid=(b // bg, t),
        in_specs=[
            step(D_MODEL),
            _const_spec((1, D_MODEL)),
            _const_spec((D_MODEL, D_FF)),
            _const_spec((D_MODEL, D_FF)),
            _const_spec((D_FF, D_MODEL)),
            _const_spec((1, D_MODEL)),
            _const_spec((D_SSM, D_MODEL)),
            _const_spec((D_MODEL, D_IN - D_SSM)),
        ],
        out_specs=(step(D_MODEL),
                   pl.BlockSpec((N_SSM_GROUPS, None, SSM_GROUP, bg * nc), lambda g, i: (0, i, 0, g)),
                   step(D_POOL), step(D_CONV)),
        compiler_params=_cparams("parallel", "parallel"),
    )(x4, g1, wg, wu, wd, g2, w_ssm_t, w_rest)


def _s5_prep_kernel(lr2_ref, li2_ref, ldt_ref, br2_ref, bi2_ref, lrc_ref, lic_ref, crt_ref, cit_ref,
                    zt_ref, wst_ref, gt_ref, cp_ref):
    hi = lax.Precision.HIGHEST
    lane = lax.broadcasted_iota(jnp.int32, (1, LANES), 1)
    first = lane < SSM_STATE

    lr2 = lr2_ref[0]
    li2 = li2_ref[0]
    dt = jnp.exp(ldt_ref[0])
    rho = lr2 * dt
    th = li2 * dt

    def pw_row(e):
        mag = jnp.exp(e * rho)
        return mag * jnp.cos(e * th), mag * jnp.sin(e * th)

    one = jnp.ones((1, 1), F32)
    a_r, a_i = pw_row(one)
    den = lr2 * lr2 + li2 * li2
    z_r = ((a_r - 1.0) * lr2 + a_i * li2) / den
    z_i = (a_i * lr2 - (a_r - 1.0) * li2) / den
    br2 = br2_ref[0]
    bi2 = bi2_ref[0]
    b1 = jnp.where(first, br2, bi2)
    b2 = jnp.where(first, -bi2, br2)
    y1 = z_r * b1 + z_i * b2
    y2 = z_r * b2 - z_i * b1

    e_lo = (S5_SUB - 1 - lax.broadcasted_iota(jnp.int32, (S5_SUB, 1), 0)).astype(F32)
    lo_r, lo_i = pw_row(e_lo)
    lob1 = jnp.concatenate([lo_r[i:i + 1] * y1 + lo_i[i:i + 1] * y2 for i in range(S5_SUB)], axis=0)
    lob2 = jnp.concatenate([lo_r[i:i + 1] * y2 - lo_i[i:i + 1] * y1 for i in range(S5_SUB)], axis=0)
    e_hi = (S5_SUB * (S5_NB - 1 - lax.broadcasted_iota(jnp.int32, (S5_NB, 1), 0))).astype(F32)
    hi_r, hi_i = pw_row(e_hi)
    pb = jnp.concatenate([hi_r[m:m + 1] * lob1 + hi_i[m:m + 1] * lob2 for m in range(S5_NB)], axis=0)
    wst_ref[0] = pb.astype(BF16)

    lrc = lrc_ref[0]
    lic = lic_ref[0]
    rho_c = lrc * dt
    th_c = lic * dt
    mag_c = jnp.exp(rho_c)
    ac_r = mag_c * jnp.cos(th_c)
    ac_i = mag_c * jnp.sin(th_c)
    rc_r, rc_i = [], []
    for half in range(MXU_TILE // LANES):
        e = (lane // SSM_GROUP + half * (LANES // SSM_GROUP)).astype(F32)
        mag = jnp.exp(e * rho_c)
        t_r = mag * jnp.cos(e * th_c)
        t_i = mag * jnp.sin(e * th_c)
        c_r = crt_ref[0, :, half * LANES:(half + 1) * LANES]
        c_i = cit_ref[0, :, half * LANES:(half + 1) * LANES]
        rc_r.append(t_r * c_r - t_i * c_i)
        rc_i.append(t_r * c_i + t_i * c_r)
    ct_stack = jnp.concatenate([crt_ref[0], -cit_ref[0]], axis=0)

    def csq(v_r, v_i):
        return v_r * v_r - v_i * v_i, 2.0 * v_r * v_i

    a16_r, a16_i = ac_r, ac_i
    for _ in range(S5_SUB.bit_length() - 1):
        a16_r, a16_i = csq(a16_r, a16_i)
    g_r, g_i = ac_r, ac_i
    for j in range(S5_NB):
        for half in range(MXU_TILE // LANES):
            lo = j * MXU_TILE + half * LANES
            gt_ref[0, :SSM_STATE, lo:lo + LANES] = (g_r * rc_r[half] - g_i * rc_i[half]).astype(BF16)
            gt_ref[0, SSM_STATE:, lo:lo + LANES] = (-(g_r * rc_i[half] + g_i * rc_r[half])).astype(BF16)
        g_r, g_i = g_r * a16_r - g_i * a16_i, g_r * a16_i + g_i * a16_r

    w_r, w_i = a16_r, a16_i
    for _ in range(S5_NB.bit_length() - 1):
        w_r, w_i = csq(w_r, w_i)
    for r in range(S5_MAX_SCAN_STEPS):
        cp_ref[0, 2 * r * SSM_STATE:(2 * r + 1) * SSM_STATE, :] = w_r
        cp_ref[0, (2 * r + 1) * SSM_STATE:(2 * r + 2) * SSM_STATE, :] = w_i
        w_r, w_i = csq(w_r, w_i)

    off_rows = (S5_CHUNK - S5_SUB) * SSM_GROUP
    if off_rows:
        rc_stack = jnp.concatenate([jnp.concatenate(rc_r, axis=1), -jnp.concatenate(rc_i, axis=1)], axis=0)
        lo_row = (S5_SUB - 1) * SSM_GROUP
        zt_off = jnp.dot(pb[lo_row:lo_row + off_rows], rc_stack, precision=hi, preferred_element_type=F32)
        zt_ref[0, :off_rows, :] = zt_off.astype(BF16)
    kall = jnp.dot(pb[off_rows:], ct_stack, precision=hi, preferred_element_type=F32)
    colblk = lax.broadcasted_iota(jnp.int32, (1, MXU_TILE), 1) // SSM_GROUP
    zd = jnp.zeros((MXU_TILE, MXU_TILE), F32)
    for jj in range(S5_SUB):
        sh = (S5_SUB - 1 - jj) * SSM_GROUP
        shifted = kall if sh == 0 else jnp.concatenate(
            [kall[sh:], jnp.zeros((sh, MXU_TILE), F32)], axis=0)
        zd = jnp.where(colblk == jj, shifted, zd)
    zt_ref[0, off_rows:, :] = zd.astype(BF16)


def _s5_prep(lam_re, lam_im, log_dt, b_re, b_im, c_re, c_im):
    g = N_SSM_GROUPS
    lr2 = jnp.tile(lam_re, (1, 2))[:, None, :]
    li2 = jnp.tile(lam_im, (1, 2))[:, None, :]
    ldt = jnp.broadcast_to(log_dt[:, None, None], (g, 1, LANES))
    br2 = jnp.tile(jnp.swapaxes(b_re, 1, 2), (1, 1, 2))
    bi2 = jnp.tile(jnp.swapaxes(b_im, 1, 2), (1, 1, 2))
    lrc = jnp.broadcast_to(lam_re[:, :, None], (g, SSM_STATE, LANES))
    lic = jnp.broadcast_to(lam_im[:, :, None], (g, SSM_STATE, LANES))
    crt = jnp.tile(jnp.swapaxes(c_re, 1, 2), (1, 1, S5_SUB))
    cit = jnp.tile(jnp.swapaxes(c_im, 1, 2), (1, 1, S5_SUB))

    def spec(*shape):
        return pl.BlockSpec((1,) + shape, lambda i: (i, 0, 0))

    cp_rows = 2 * S5_MAX_SCAN_STEPS * SSM_STATE
    return pl.pallas_call(
        _s5_prep_kernel,
        out_shape=(jax.ShapeDtypeStruct((g, S5_ROWS, MXU_TILE), BF16),
                   jax.ShapeDtypeStruct((g, S5_ROWS, LANES), BF16),
                   jax.ShapeDtypeStruct((g, LANES, S5_ROWS), BF16),
                   jax.ShapeDtypeStruct((g, cp_rows, LANES), F32)),
        grid=(g,),
        in_specs=[spec(1, LANES), spec(1, LANES), spec(1, LANES),
                  spec(SSM_GROUP, LANES), spec(SSM_GROUP, LANES),
                  spec(SSM_STATE, LANES), spec(SSM_STATE, LANES),
                  spec(SSM_STATE, MXU_TILE), spec(SSM_STATE, MXU_TILE)],
        out_specs=(spec(S5_ROWS, MXU_TILE), spec(S5_ROWS, LANES), spec(LANES, S5_ROWS),
                   spec(cp_rows, LANES)),
        compiler_params=_cparams("parallel"),
    )(lr2, li2, ldt, br2, bi2, lrc, lic, crt, cit)


def _s5_main_kernel(u_ref, zw_ref, wst_ref, g_ref, cp_ref, o_ref, *, n_chunks):
    ncols = u_ref.shape[-1]
    u = u_ref[0].reshape(S5_ROWS, ncols).astype(BF16)
    state = jnp.dot(wst_ref[0], u, preferred_element_type=F32)
    s_r, s_i = state[:SSM_STATE], state[SSM_STATE:]
    chunk = lax.broadcasted_iota(jnp.int32, (1, ncols), 1) % n_chunks
    reps = ncols // LANES if ncols > LANES else 1

    def wide(v):
        return jnp.concatenate([v] * reps, axis=1)[:, :ncols]

    r = 0
    while (1 << r) < n_chunks:
        s = 1 << r
        p_r = jnp.where(chunk >= s, pltpu.roll(s_r, s, 1), 0.0)
        p_i = jnp.where(chunk >= s, pltpu.roll(s_i, s, 1), 0.0)
        w_r = wide(cp_ref[0, 2 * r * SSM_STATE:(2 * r + 1) * SSM_STATE, :])
        w_i = wide(cp_ref[0, (2 * r + 1) * SSM_STATE:(2 * r + 2) * SSM_STATE, :])
        s_r, s_i = s_r + w_r * p_r - w_i * p_i, s_i + w_r * p_i + w_i * p_r
        r += 1
    c_r = jnp.where(chunk >= 1, pltpu.roll(s_r, 1, 1), 0.0)
    c_i = jnp.where(chunk >= 1, pltpu.roll(s_i, 1, 1), 0.0)
    carry = jnp.concatenate([c_r, c_i], axis=0).astype(BF16)
    for j in range(S5_NB):
        k = (j + 1) * MXU_TILE
        y = jnp.dot(zw_ref[0, :, (S5_NB - 1 - j) * MXU_TILE:], u[:k], preferred_element_type=F32)
        y = y + jnp.dot(g_ref[0, j * MXU_TILE:(j + 1) * MXU_TILE, :], carry, preferred_element_type=F32)
        o_ref[0, j * S5_SUB:(j + 1) * S5_SUB] = y.reshape(S5_SUB, SSM_GROUP, ncols)


def _s5_main(u4, zw, wst, gm, cp, *, n_chunks):
    g, t, _, ncols = u4.shape
    cp_rows = cp.shape[1]

    def spec(*shape):
        return pl.BlockSpec((1,) + shape, lambda i: (i,) + (0,) * len(shape))

    return pl.pallas_call(
        functools.partial(_s5_main_kernel, n_chunks=n_chunks),
        out_shape=jax.ShapeDtypeStruct((g, t, SSM_GROUP, ncols), F32),
        grid=(g,),
        in_specs=[spec(t, SSM_GROUP, ncols), spec(MXU_TILE, S5_ROWS), spec(LANES, S5_ROWS),
                  spec(S5_ROWS, LANES), spec(cp_rows, LANES)],
        out_specs=spec(t, SSM_GROUP, ncols),
        compiler_params=_cparams("parallel"),
    )(u4, zw, wst, gm, cp)


def _kv_kernel(m_ref, g_ref, wk_ref, wv_ref, k_ref, v_ref):
    m = _rms(m_ref[...], g_ref[...]).astype(BF16)
    k_ref[...] = jnp.dot(m, wk_ref[...], preferred_element_type=F32).astype(BF16)
    v_ref[...] = jnp.dot(m, wv_ref[...], preferred_element_type=F32).astype(BF16)


def _kv(mem2d, norm_g, wk, wv, *, tm):
    n = mem2d.shape[0]
    row = pl.BlockSpec((tm, D_MODEL), lambda i: (i, 0))
    return pl.pallas_call(
        _kv_kernel,
        out_shape=(jax.ShapeDtypeStruct((n, D_MODEL), BF16), jax.ShapeDtypeStruct((n, D_MODEL), BF16)),
        grid=(n // tm,),
        in_specs=[row, _const_spec((1, D_MODEL)), _const_spec((D_MODEL, D_MODEL)),
                  _const_spec((D_MODEL, D_MODEL))],
        out_specs=(row, row),
        compiler_params=_cparams("parallel"),
    )(mem2d, norm_g, wk, wv)


def _fill_history(ext_ref, src_ref, pad, chunk):
    t = src_ref.shape[1]
    ext_ref[pad:] = src_ref[0]
    for e in range(pad):
        back = e // t + 1
        v = src_ref[0, t - 1 - e % t]
        ext_ref[pad - 1 - e] = jnp.where(chunk >= back, pltpu.roll(v, back, 0), 0.0)


def _pool_conv_tile(uext_ref, hext_ref, p_ref, c_ref, pw_ref, ps_ref, cw_ref, cb_ref, lg_ref, lb_ref,
                    step0, ts, t):
    nc = uext_ref.shape[1]
    chunk = lax.broadcasted_iota(jnp.int32, (nc, 1), 0)
    low = lax.broadcasted_iota(jnp.int32, (1, LANES), 1) < POOL_GROUP
    first_tap = CONV_PAD - (CONV_WIDTH - 1)
    groups_per_tile = LANES // POOL_GROUP

    def one_step(s, carry):
        i = step0 + s
        rows = pl.ds(pl.multiple_of(s * nc, nc), nc)
        pos = (chunk * t + i + 1).astype(F32)
        for lt in range(D_POOL // LANES):
            lanes = slice(lt * LANES, (lt + 1) * LANES)
            w_lo, w_hi = POOL_WINDOWS[groups_per_tile * lt:groups_per_tile * (lt + 1)]
            cur = uext_ref[POOL_PAD + i, :, lanes]
            run = cur
            saved = {}
            for d in range(1, w_hi):
                run = run + uext_ref[POOL_PAD + i - d, :, lanes]
                if d + 1 in (w_lo, w_hi):
                    saved[d + 1] = run
            total = jnp.where(low, saved[w_lo], saved[w_hi])
            count = jnp.minimum(pos, jnp.where(low, float(w_lo), float(w_hi)))
            p_ref[rows, lanes] = total / count - cur
        for lt in range(D_CONV // LANES):
            lanes = slice(lt * LANES, (lt + 1) * LANES)
            acc = jnp.zeros((nc, LANES), F32) + cb_ref[:, lanes]
            for k in range(CONV_WIDTH):
                acc = acc + cw_ref[k:k + 1, lanes] * hext_ref[i + first_tap + k, :, lanes]
            c_ref[rows, lanes] = acc
        return carry

    lax.fori_loop(0, ts, one_step, 0)
    y_pool = jnp.dot(p_ref[...].astype(BF16), pw_ref[...], preferred_element_type=F32) * ps_ref[...]
    conv = c_ref[...]
    mu = jnp.mean(conv, axis=-1, keepdims=True)
    cen = conv - mu
    var = jnp.mean(cen * cen, axis=-1, keepdims=True)
    hn = cen * lax.rsqrt(var + EPS) * lg_ref[...] + lb_ref[...]
    return jnp.concatenate([y_pool, hn * jax.nn.sigmoid(hn)], axis=1).astype(BF16)


def _mix_attn_kernel(x_ref, y4_ref, u4_ref, up_ref, hc_ref, k_ref, v_ref,
                     pw_ref, ps_ref, cw_ref, cb_ref, lg_ref, lb_ref,
                     d_ref, wglut_ref, wo1_ref, wo2_ref, ga_ref, wq_ref, wo_ref,
                     o_ref, uext_ref, hext_ref, p_ref, c_ref):
    _, ts, nc, _ = x_ref.shape
    t = up_ref.shape[1]
    j = pl.program_id(1)
    chunk = lax.broadcasted_iota(jnp.int32, (nc, 1), 0)

    @pl.when(j == 0)
    def _():
        _fill_history(uext_ref, up_ref, POOL_PAD, chunk)
        _fill_history(hext_ref, hc_ref, CONV_PAD, chunk)

    y_pc = _pool_conv_tile(uext_ref, hext_ref, p_ref, c_ref, pw_ref, ps_ref, cw_ref, cb_ref, lg_ref, lb_ref,
                           j * ts, ts, t)

    yt = jnp.concatenate([y4_ref[:, s].reshape(D_SSM, nc) for s in range(ts)], axis=1)
    ut = jnp.concatenate([u4_ref[:, s].reshape(D_SSM, nc) for s in range(ts)], axis=1)
    yt = jax.nn.gelu(yt + d_ref[...] * ut, approximate=True)
    gate = jax.nn.sigmoid(jnp.dot(wglut_ref[...], yt.astype(BF16), preferred_element_type=F32))
    y_ssm = (yt * gate).T.astype(BF16)

    x = x_ref[0].reshape(ts * nc, D_MODEL)
    x = x + jnp.dot(y_ssm, wo1_ref[...], preferred_element_type=F32) \
        + jnp.dot(y_pc, wo2_ref[...], preferred_element_type=F32)

    h = _rms(x, ga_ref[...]).astype(BF16)
    q = jnp.dot(h, wq_ref[...], preferred_element_type=F32) * (XHEAD_DIM ** -0.5)
    heads = []
    for hd in range(N_XHEADS):
        sl = slice(hd * XHEAD_DIM, (hd + 1) * XHEAD_DIM)
        sc = lax.dot_general(q[:, sl].astype(BF16), k_ref[0, :, sl], (((1,), (1,)), ((), ())),
                             preferred_element_type=F32)
        pr = jnp.exp(sc - jnp.max(sc, axis=-1, keepdims=True))
        l = jnp.sum(pr, axis=-1, keepdims=True)
        heads.append(jnp.dot(pr.astype(BF16), v_ref[0, :, sl], preferred_element_type=F32) / l)
    o = jnp.concatenate(heads, axis=-1).astype(BF16)
    x = x + jnp.dot(o, wo_ref[...], preferred_element_type=F32)
    o_ref[0] = x.reshape(ts, nc, D_MODEL)


def _mix_attn(x4, y4, u4, up4, hc4, k3d, v3d, pool_w_bd, pool_scale, conv_w, conv_b, ln_g, ln_b,
              d_col, w_glu_t, w_out1, w_out2, ga, wq, wo, *, ts):
    b, t, nc, _ = x4.shape
    xspec = pl.BlockSpec((1, ts, nc, D_MODEL), lambda i, j: (i, j, 0, 0))
    chan = pl.BlockSpec((N_SSM_GROUPS, ts, SSM_GROUP, nc), lambda i, j: (0, j, 0, i))
    whole = lambda w: pl.BlockSpec((1, t, nc, w), lambda i, j: (i, 0, 0, 0))
    mem = pl.BlockSpec((1, MEM_LEN, D_MODEL), lambda i, j: (i, 0, 0))
    return pl.pallas_call(
        _mix_attn_kernel,
        out_shape=jax.ShapeDtypeStruct((b, t, nc, D_MODEL), F32),
        grid=(b, t // ts),
        in_specs=[
            xspec, chan, chan, whole(D_POOL), whole(D_CONV), mem, mem,
            _const_spec((D_POOL, D_POOL)), _const_spec((1, D_POOL)),
            _const_spec((CONV_WIDTH, D_CONV)), _const_spec((1, D_CONV)),
            _const_spec((1, D_CONV)), _const_spec((1, D_CONV)),
            _const_spec((D_SSM, 1)), _const_spec((D_SSM, D_SSM)),
            _const_spec((D_SSM, D_MODEL)), _const_spec((D_PC, D_MODEL)),
            _const_spec((1, D_MODEL)), _const_spec((D_MODEL, D_MODEL)), _const_spec((D_MODEL, D_MODEL)),
        ],
        out_specs=xspec,
        scratch_shapes=[pltpu.VMEM((POOL_PAD + t, nc, D_POOL), F32),
                        pltpu.VMEM((CONV_PAD + t, nc, D_CONV), F32),
                        pltpu.VMEM((ts * nc, D_POOL), F32),
                        pltpu.VMEM((ts * nc, D_CONV), F32)],
        compiler_params=_cparams("parallel", "arbitrary"),
    )(x4, y4, u4, up4, hc4, k3d, v3d, pool_w_bd, pool_scale, conv_w, conv_b, ln_g, ln_b,
      d_col, w_glu_t, w_out1, w_out2, ga, wq, wo)


def _ffn_kernel(x_ref, g_ref, wg_ref, wu_ref, wd_ref, fg_ref, o_ref, *, final):
    y = _ffn_body(x_ref[...], g_ref, wg_ref, wu_ref, wd_ref)
    if final:
        y = _rms(y, fg_ref[...])
    o_ref[...] = y


def _ffn(x2d, norm_g, wg, wu, wd, final_g, *, final, tm):
    n = x2d.shape[0]
    row = pl.BlockSpec((tm, D_MODEL), lambda i: (i, 0))
    return pl.pallas_call(
        functools.partial(_ffn_kernel, final=final),
        out_shape=jax.ShapeDtypeStruct((n, D_MODEL), F32),
        grid=(n // tm,),
        in_specs=[row, _const_spec((1, D_MODEL)), _const_spec((D_MODEL, D_FF)),
                  _const_spec((D_MODEL, D_FF)), _const_spec((D_FF, D_MODEL)), _const_spec((1, D_MODEL))],
        out_specs=row,
        compiler_params=_cparams("parallel"),
    )(x2d, norm_g, wg, wu, wd, final_g)


def _tile(n, pref):
    t = max(1, min(n, pref))
    assert n % t == 0, (n, t)
    return t


def kernel(x, mem, ffn1_norm, ffn1_w_gate, ffn1_w_up, ffn1_w_down, mix_norm, w_in, w_out, ssm_lambda_re, ssm_lambda_im, ssm_log_dt, ssm_b_re, ssm_b_im, ssm_c_re, ssm_c_im, ssm_d, ssm_w_glu, pool_w, pool_scale, conv_w, conv_b, conv_ln_g, conv_ln_b, xattn_norm, mem_norm, xattn_wq, xattn_wk, xattn_wv, xattn_wo, ffn2_norm, ffn2_w_gate, ffn2_w_up, ffn2_w_down, final_norm):
    bsz, seq, _ = x.shape
    depth = w_in.shape[0]
    n = bsz * seq
    assert seq % S5_CHUNK == 0
    n_chunks = seq // S5_CHUNK
    assert n_chunks & (n_chunks - 1) == 0 and n_chunks <= (1 << S5_MAX_SCAN_STEPS)
    bg = _tile(bsz, TOKEN_TILE // n_chunks)
    ts = _tile(S5_CHUNK, TOKEN_TILE // n_chunks)
    tm = _tile(n, TOKEN_TILE)
    tkv = _tile(bsz * MEM_LEN, TOKEN_TILE)
    bf = lambda w: w.astype(BF16)
    row = lambda v: v.reshape(1, -1)

    x4 = x.reshape(bsz, n_chunks, S5_CHUNK, D_MODEL).swapaxes(1, 2)
    mem2d = mem.reshape(bsz * MEM_LEN, D_MODEL)
    for l in range(depth):
        x4, u4, up4, hc4 = _ffn_mix_in(
            x4, row(ffn1_norm[l]), bf(ffn1_w_gate[l]), bf(ffn1_w_up[l]), bf(ffn1_w_down[l]),
            row(mix_norm[l]), bf(w_in[l, :, :D_SSM].T), bf(w_in[l, :, D_SSM:]), bg=bg)

        zt, wst_t, gt, cp = _s5_prep(ssm_lambda_re[l], ssm_lambda_im[l], ssm_log_dt[l], ssm_b_re[l],
                                     ssm_b_im[l], ssm_c_re[l], ssm_c_im[l])
        y4 = _s5_main(u4, jnp.swapaxes(zt, 1, 2), jnp.swapaxes(wst_t, 1, 2), jnp.swapaxes(gt, 1, 2), cp,
                      n_chunks=n_chunks)

        k2d, v2d = _kv(mem2d, row(mem_norm[l]), bf(xattn_wk[l]), bf(xattn_wv[l]), tm=tkv)
        pool_bd = jax.scipy.linalg.block_diag(*[pool_w[l, g] for g in range(len(POOL_WINDOWS))])
        x4 = _mix_attn(x4, y4, u4, up4, hc4, k2d.reshape(bsz, MEM_LEN, D_MODEL),
                       v2d.reshape(bsz, MEM_LEN, D_MODEL), bf(pool_bd), row(pool_scale[l]), conv_w[l],
                       row(conv_b[l]), row(conv_ln_g[l]), row(conv_ln_b[l]), ssm_d[l].reshape(D_SSM, 1),
                       bf(ssm_w_glu[l].T), bf(w_out[l, :D_SSM]), bf(w_out[l, D_SSM:]),
                       row(xattn_norm[l]), bf(xattn_wq[l]), bf(xattn_wo[l]), ts=ts)

        x2d = _ffn(x4.reshape(n, D_MODEL), row(ffn2_norm[l]), bf(ffn2_w_gate[l]), bf(ffn2_w_up[l]),
                   bf(ffn2_w_down[l]), row(final_norm), final=(l == depth - 1), tm=tm)
        x4 = x2d.reshape(bsz, S5_CHUNK, n_chunks, D_MODEL)
    return x4.swapaxes(1, 2).reshape(bsz, seq, D_MODEL)
```

```python
import functools

import jax
import jax.numpy as jnp
from jax import lax
from jax.experimental import pallas as pl
from jax.experimental.pallas import tpu as pltpu

F32 = jnp.float32
BF16 = jnp.bfloat16

D_MODEL = 1024
MEM_LEN = 256
D_SSM = 384
D_POOL = 256
D_CONV = 384
D_PC = D_POOL + D_CONV
SSM_GROUP = 16
N_SSM_GROUPS = 24
SSM_STATE = 64
POOL_WINDOWS = (2, 4, 8, 16)
POOL_GROUP = 64
CONV_WIDTH = 31
D_IN = D_SSM + D_POOL + 2 * D_CONV
D_FF = 2816
N_XHEADS = 4
XHEAD_DIM = 256
EPS = 1e-6

SUBLANES = 8
LANES = 128
MXU_TILE = 256

TOKEN_TILE = 512
FF_CHUNK = MXU_TILE
S5_SUB = MXU_TILE // SSM_GROUP
S5_CHUNK = S5_SUB
S5_NB = S5_CHUNK // S5_SUB
S5_ROWS = S5_CHUNK * SSM_GROUP
S5_MAX_SCAN_STEPS = 7
CONV_PAD = 32
POOL_PAD = 16
VMEM_LIMIT = 56 * 1024 * 1024


def _cparams(*sem):
    return pltpu.CompilerParams(dimension_semantics=sem, vmem_limit_bytes=VMEM_LIMIT)


def _const_spec(shape):
    nd = len(shape)
    return pl.BlockSpec(shape, lambda *_: (0,) * nd, pipeline_mode=pl.Buffered(1))


def _layer_spec(shape, layer):
    nd = len(shape)
    return pl.BlockSpec((None,) + shape, lambda *_: (layer,) + (0,) * nd, pipeline_mode=pl.Buffered(1))


def _rms(x, g):
    ms = jnp.mean(x * x, axis=-1, keepdims=True)
    return x * lax.rsqrt(ms + EPS) * g


def _ffn_body(x, g_ref, wg_ref, wu_ref, wd_ref):
    h = _rms(x, g_ref[...]).astype(BF16)
    acc = jnp.zeros(x.shape, F32)
    for c in range(D_FF // FF_CHUNK):
        sl = slice(c * FF_CHUNK, (c + 1) * FF_CHUNK)
        gate = jnp.dot(h, wg_ref[:, sl], preferred_element_type=F32)
        up = jnp.dot(h, wu_ref[:, sl], preferred_element_type=F32)
        act = (gate * jax.nn.sigmoid(gate) * up).astype(BF16)
        acc = acc + jnp.dot(act, wd_ref[sl, :], preferred_element_type=F32)
    return x + 0.5 * acc


def _ffn_mix_in_kernel(x_ref, g1_ref, wg_ref, wu_ref, wd_ref, g2_ref, wst_ref, wrest_ref,
                       xo_ref, u4_ref, up_ref, hc_ref):
    b, nc, _ = x_ref.shape
    y = _ffn_body(x_ref[...].reshape(b * nc, D_MODEL), g1_ref, wg_ref, wu_ref, wd_ref)
    xo_ref[...] = y.reshape(b, nc, D_MODEL)
    h = _rms(y, g2_ref[...]).astype(BF16)
    ut = lax.dot_general(wst_ref[...], h, (((1,), (1,)), ((), ())), preferred_element_type=F32)
    u4_ref[...] = ut.reshape(N_SSM_GROUPS, SSM_GROUP, b * nc)
    z = jnp.dot(h, wrest_ref[...], preferred_element_type=F32)
    up_ref[...] = z[:, :D_POOL].reshape(b, nc, D_POOL)
    v = z[:, D_POOL:D_POOL + D_CONV]
    g = z[:, D_POOL + D_CONV:]
    hc_ref[...] = (v * jax.nn.sigmoid(g)).reshape(b, nc, D_CONV)


def _ffn_mix_in_tm_kernel(x_ref, g1_ref, wg_ref, wu_ref, wd_ref, g2_ref, wst_ref, wrest_ref,
                          xo_ref, u4_ref, up_ref, hc_ref):
    _, nc, ts, _ = x_ref.shape
    x = jnp.swapaxes(x_ref[0], 0, 1).reshape(ts * nc, D_MODEL)
    y = _ffn_body(x, g1_ref, wg_ref, wu_ref, wd_ref)
    xo_ref[0] = y.reshape(ts, nc, D_MODEL)
    h = _rms(y, g2_ref[...]).astype(BF16)
    ut = lax.dot_general(wst_ref[...], h, (((1,), (1,)), ((), ())), preferred_element_type=F32)
    for s in range(ts):
        u4_ref[:, s] = ut[:, s * nc:(s + 1) * nc].reshape(N_SSM_GROUPS, SSM_GROUP, nc)
    z = jnp.dot(h, wrest_ref[...], preferred_element_type=F32)
    up_ref[0] = z[:, :D_POOL].reshape(ts, nc, D_POOL)
    v = z[:, D_POOL:D_POOL + D_CONV]
    g = z[:, D_POOL + D_CONV:]
    hc_ref[0] = (v * jax.nn.sigmoid(g)).reshape(ts, nc, D_CONV)


def _ffn_mix_in_tm(x_tm, g1, wg, wu, wd, g2, w_ssm_t, w_rest, *, layer, ts):
    b, nc, t, _ = x_tm.shape
    tile = lambda w: pl.BlockSpec((1, ts, nc, w), lambda i, j: (i, j, 0, 0))
    return pl.pallas_call(
        _ffn_mix_in_tm_kernel,
        out_shape=(jax.ShapeDtypeStruct((b, t, nc, D_MODEL), F32),
                   jax.ShapeDtypeStruct((N_SSM_GROUPS, t, SSM_GROUP, b * nc), F32),
                   jax.ShapeDtypeStruct((b, t, nc, D_POOL), F32),
                   jax.ShapeDtypeStruct((b, t, nc, D_CONV), F32)),
        grid=(b, t // ts),
        in_specs=[
            pl.BlockSpec((1, nc, ts, D_MODEL), lambda i, j: (i, 0, j, 0)),
            _const_spec((1, D_MODEL)),
            _layer_spec((D_MODEL, D_FF), layer),
            _layer_spec((D_MODEL, D_FF), layer),
            _layer_spec((D_FF, D_MODEL), layer),
            _const_spec((1, D_MODEL)),
            _const_spec((D_SSM, D_MODEL)),
            _const_spec((D_MODEL, D_IN - D_SSM)),
        ],
        out_specs=(tile(D_MODEL),
                   pl.BlockSpec((N_SSM_GROUPS, ts, SSM_GROUP, nc), lambda i, j: (0, j, 0, i)),
                   tile(D_POOL), tile(D_CONV)),
        compiler_params=_cparams("parallel", "parallel"),
    )(x_tm, g1, wg, wu, wd, g2, w_ssm_t, w_rest)


def _ffn_mix_in(x4, g1, wg, wu, wd, g2, w_ssm_t, w_rest, *, layer, bg):
    b, t, nc, _ = x4.shape
    step = lambda w: pl.BlockSpec((bg, None, nc, w), lambda g, i: (g, i, 0, 0))
    return pl.pallas_call(
        _ffn_mix_in_kernel,
        out_shape=(jax.ShapeDtypeStruct((b, t, nc, D_MODEL), F32),
                   jax.ShapeDtypeStruct((N_SSM_GROUPS, t, SSM_GROUP, b * nc), F32),
                   jax.ShapeDtypeStruct((b, t, nc, D_POOL), F32),
                   jax.ShapeDtypeStruct((b, t, nc, D_CONV), F32)),
        grid=(b // bg, t),
        in_specs=[
            step(D_MODEL),
            _const_spec((1, D_MODEL)),
            _layer_spec((D_MODEL, D_FF), layer),
            _layer_spec((D_MODEL, D_FF), layer),
            _layer_spec((D_FF, D_MODEL), layer),
            _const_spec((1, D_MODEL)),
            _const_spec((D_SSM, D_MODEL)),
            _const_spec((D_MODEL, D_IN - D_SSM)),
        ],
        out_specs=(step(D_MODEL),
                   pl.BlockSpec((N_SSM_GROUPS, None, SSM_GROUP, bg * nc), lambda g, i: (0, i, 0, g)),
                   step(D_POOL), step(D_CONV)),
        compiler_params=_cparams("parallel", "parallel"),
    )(x4, g1, wg, wu, wd, g2, w_ssm_t, w_rest)


def _s5_prep_kernel(lr2_ref, li2_ref, ldt_ref, br2_ref, bi2_ref, lrc_ref, lic_ref, crt_ref, cit_ref,
                    zt_ref, wst_ref, gt_ref, cp_ref):
    hi = lax.Precision.HIGHEST
    lane = lax.broadcasted_iota(jnp.int32, (1, LANES), 1)
    first = lane < SSM_STATE

    lr2 = lr2_ref[0]
    li2 = li2_ref[0]
    dt = jnp.exp(ldt_ref[0])
    rho = lr2 * dt
    th = li2 * dt

    def pw_row(e):
        mag = jnp.exp(e * rho)
        return mag * jnp.cos(e * th), mag * jnp.sin(e * th)

    one = jnp.ones((1, 1), F32)
    a_r, a_i = pw_row(one)
    den = lr2 * lr2 + li2 * li2
    z_r = ((a_r - 1.0) * lr2 + a_i * li2) / den
    z_i = (a_i * lr2 - (a_r - 1.0) * li2) / den
    br2 = br2_ref[0]
    bi2 = bi2_ref[0]
    b1 = jnp.where(first, br2, bi2)
    b2 = jnp.where(first, -bi2, br2)
    y1 = z_r * b1 + z_i * b2
    y2 = z_r * b2 - z_i * b1

    e_lo = (S5_SUB - 1 - lax.broadcasted_iota(jnp.int32, (S5_SUB, 1), 0)).astype(F32)
    lo_r, lo_i = pw_row(e_lo)
    lob1 = jnp.concatenate([lo_r[i:i + 1] * y1 + lo_i[i:i + 1] * y2 for i in range(S5_SUB)], axis=0)
    lob2 = jnp.concatenate([lo_r[i:i + 1] * y2 - lo_i[i:i + 1] * y1 for i in range(S5_SUB)], axis=0)
    e_hi = (S5_SUB * (S5_NB - 1 - lax.broadcasted_iota(jnp.int32, (S5_NB, 1), 0))).astype(F32)
    hi_r, hi_i = pw_row(e_hi)
    pb = jnp.concatenate([hi_r[m:m + 1] * lob1 + hi_i[m:m + 1] * lob2 for m in range(S5_NB)], axis=0)
    wst_ref[0] = pb.astype(BF16)

    lrc = lrc_ref[0]
    lic = lic_ref[0]
    rho_c = lrc * dt
    th_c = lic * dt
    mag_c = jnp.exp(rho_c)
    ac_r = mag_c * jnp.cos(th_c)
    ac_i = mag_c * jnp.sin(th_c)
    rc_r, rc_i = [], []
    for half in range(MXU_TILE // LANES):
        e = (lane // SSM_GROUP + half * (LANES // SSM_GROUP)).astype(F32)
        mag = jnp.exp(e * rho_c)
        t_r = mag * jnp.cos(e * th_c)
        t_i = mag * jnp.sin(e * th_c)
        c_r = crt_ref[0, :, half * LANES:(half + 1) * LANES]
        c_i = cit_ref[0, :, half * LANES:(half + 1) * LANES]
        rc_r.append(t_r * c_r - t_i * c_i)
        rc_i.append(t_r * c_i + t_i * c_r)
    ct_stack = jnp.concatenate([crt_ref[0], -cit_ref[0]], axis=0)

    def csq(v_r, v_i):
        return v_r * v_r - v_i * v_i, 2.0 * v_r * v_i

    a16_r, a16_i = ac_r, ac_i
    for _ in range(S5_SUB.bit_length() - 1):
        a16_r, a16_i = csq(a16_r, a16_i)
    g_r, g_i = ac_r, ac_i
    for j in range(S5_NB):
        for half in range(MXU_TILE // LANES):
            lo = j * MXU_TILE + half * LANES
            gt_ref[0, :SSM_STATE, lo:lo + LANES] = (g_r * rc_r[half] - g_i * rc_i[half]).astype(BF16)
            gt_ref[0, SSM_STATE:, lo:lo + LANES] = (-(g_r * rc_i[half] + g_i * rc_r[half])).astype(BF16)
        g_r, g_i = g_r * a16_r - g_i * a16_i, g_r * a16_i + g_i * a16_r

    w_r, w_i = a16_r, a16_i
    for _ in range(S5_NB.bit_length() - 1):
        w_r, w_i = csq(w_r, w_i)
    for r in range(S5_MAX_SCAN_STEPS):
        cp_ref[0, 2 * r * SSM_STATE:(2 * r + 1) * SSM_STATE, :] = w_r
        cp_ref[0, (2 * r + 1) * SSM_STATE:(2 * r + 2) * SSM_STATE, :] = w_i
        w_r, w_i = csq(w_r, w_i)

    off_rows = (S5_CHUNK - S5_SUB) * SSM_GROUP
    if off_rows:
        rc_stack = jnp.concatenate([jnp.concatenate(rc_r, axis=1), -jnp.concatenate(rc_i, axis=1)], axis=0)
        lo_row = (S5_SUB - 1) * SSM_GROUP
        zt_off = jnp.dot(pb[lo_row:lo_row + off_rows], rc_stack, precision=hi, preferred_element_type=F32)
        zt_ref[0, :off_rows, :] = zt_off.astype(BF16)
    kall = jnp.dot(pb[off_rows:], ct_stack, precision=hi, preferred_element_type=F32)
    colblk = lax.broadcasted_iota(jnp.int32, (1, MXU_TILE), 1) // SSM_GROUP
    zd = jnp.zeros((MXU_TILE, MXU_TILE), F32)
    for jj in range(S5_SUB):
        sh = (S5_SUB - 1 - jj) * SSM_GROUP
        shifted = kall if sh == 0 else jnp.concatenate(
            [kall[sh:], jnp.zeros((sh, MXU_TILE), F32)], axis=0)
        zd = jnp.where(colblk == jj, shifted, zd)
    zt_ref[0, off_rows:, :] = zd.astype(BF16)


def _s5_prep(lam_re, lam_im, log_dt, b_re, b_im, c_re, c_im):
    g = N_SSM_GROUPS
    lr2 = jnp.tile(lam_re, (1, 2))[:, None, :]
    li2 = jnp.tile(lam_im, (1, 2))[:, None, :]
    ldt = jnp.broadcast_to(log_dt[:, None, None], (g, 1, LANES))
    br2 = jnp.tile(jnp.swapaxes(b_re, 1, 2), (1, 1, 2))
    bi2 = jnp.tile(jnp.swapaxes(b_im, 1, 2), (1, 1, 2))
    lrc = jnp.broadcast_to(lam_re[:, :, None], (g, SSM_STATE, LANES))
    lic = jnp.broadcast_to(lam_im[:, :, None], (g, SSM_STATE, LANES))
    crt = jnp.tile(jnp.swapaxes(c_re, 1, 2), (1, 1, S5_SUB))
    cit = jnp.tile(jnp.swapaxes(c_im, 1, 2), (1, 1, S5_SUB))

    def spec(*shape):
        return pl.BlockSpec((1,) + shape, lambda i: (i, 0, 0))

    cp_rows = 2 * S5_MAX_SCAN_STEPS * SSM_STATE
    return pl.pallas_call(
        _s5_prep_kernel,
        out_shape=(jax.ShapeDtypeStruct((g, S5_ROWS, MXU_TILE), BF16),
                   jax.ShapeDtypeStruct((g, S5_ROWS, LANES), BF16),
                   jax.ShapeDtypeStruct((g, LANES, S5_ROWS), BF16),
                   jax.ShapeDtypeStruct((g, cp_rows, LANES), F32)),
        grid=(g,),
        in_specs=[spec(1, LANES), spec(1, LANES), spec(1, LANES),
                  spec(SSM_GROUP, LANES), spec(SSM_GROUP, LANES),
                  spec(SSM_STATE, LANES), spec(SSM_STATE, LANES),
                  spec(SSM_STATE, MXU_TILE), spec(SSM_STATE, MXU_TILE)],
        out_specs=(spec(S5_ROWS, MXU_TILE), spec(S5_ROWS, LANES), spec(LANES, S5_ROWS),
                   spec(cp_rows, LANES)),
        compiler_params=_cparams("parallel"),
    )(lr2, li2, ldt, br2, bi2, lrc, lic, crt, cit)


def _s5_main_kernel(u_ref, zw_ref, wst_ref, g_ref, cp_ref, o_ref, *, n_chunks):
    ncols = u_ref.shape[-1]
    u = u_ref[0].reshape(S5_ROWS, ncols).astype(BF16)
    state = jnp.dot(wst_ref[0], u, preferred_element_type=F32)
    s_r, s_i = state[:SSM_STATE], state[SSM_STATE:]
    chunk = lax.broadcasted_iota(jnp.int32, (1, ncols), 1) % n_chunks
    reps = ncols // LANES if ncols > LANES else 1

    def wide(v):
        return jnp.concatenate([v] * reps, axis=1)[:, :ncols]

    r = 0
    while (1 << r) < n_chunks:
        s = 1 << r
        p_r = jnp.where(chunk >= s, pltpu.roll(s_r, s, 1), 0.0)
        p_i = jnp.where(chunk >= s, pltpu.roll(s_i, s, 1), 0.0)
        w_r = wide(cp_ref[0, 2 * r * SSM_STATE:(2 * r + 1) * SSM_STATE, :])
        w_i = wide(cp_ref[0, (2 * r + 1) * SSM_STATE:(2 * r + 2) * SSM_STATE, :])
        s_r, s_i = s_r + w_r * p_r - w_i * p_i, s_i + w_r * p_i + w_i * p_r
        r += 1
    c_r = jnp.where(chunk >= 1, pltpu.roll(s_r, 1, 1), 0.0)
    c_i = jnp.where(chunk >= 1, pltpu.roll(s_i, 1, 1), 0.0)
    carry = jnp.concatenate([c_r, c_i], axis=0).astype(BF16)
    for j in range(S5_NB):
        k = (j + 1) * MXU_TILE
        y = jnp.dot(zw_ref[0, :, (S5_NB - 1 - j) * MXU_TILE:], u[:k], preferred_element_type=F32)
        y = y + jnp.dot(g_ref[0, j * MXU_TILE:(j + 1) * MXU_TILE, :], carry, preferred_element_type=F32)
        o_ref[0, j * S5_SUB:(j + 1) * S5_SUB] = y.reshape(S5_SUB, SSM_GROUP, ncols)


def _s5_main(u4, zw, wst, gm, cp, *, n_chunks):
    g, t, _, ncols = u4.shape
    cp_rows = cp.shape[1]

    def spec(*shape):
        return pl.BlockSpec((1,) + shape, lambda i: (i,) + (0,) * len(shape))

    return pl.pallas_call(
        functools.partial(_s5_main_kernel, n_chunks=n_chunks),
        out_shape=jax.ShapeDtypeStruct((g, t, SSM_GROUP, ncols), F32),
        grid=(g,),
        in_specs=[spec(t, SSM_GROUP, ncols), spec(MXU_TILE, S5_ROWS), spec(LANES, S5_ROWS),
                  spec(S5_ROWS, LANES), spec(cp_rows, LANES)],
        out_specs=spec(t, SSM_GROUP, ncols),
        compiler_params=_cparams("parallel"),
    )(u4, zw, wst, gm, cp)


def _kv_kernel(m_ref, g_ref, wk_ref, wv_ref, k_ref, v_ref):
    m = _rms(m_ref[...], g_ref[...]).astype(BF16)
    k_ref[...] = jnp.dot(m, wk_ref[...], preferred_element_type=F32).astype(BF16)
    v_ref[...] = jnp.dot(m, wv_ref[...], preferred_element_type=F32).astype(BF16)


def _kv(mem2d, norm_g, wk, wv, *, layer, tm):
    n = mem2d.shape[0]
    row = pl.BlockSpec((tm, D_MODEL), lambda i: (i, 0))
    return pl.pallas_call(
        _kv_kernel,
        out_shape=(jax.ShapeDtypeStruct((n, D_MODEL), BF16), jax.ShapeDtypeStruct((n, D_MODEL), BF16)),
        grid=(n // tm,),
        in_specs=[row, _const_spec((1, D_MODEL)), _layer_spec((D_MODEL, D_MODEL), layer),
                  _layer_spec((D_MODEL, D_MODEL), layer)],
        out_specs=(row, row),
        compiler_params=_cparams("parallel"),
    )(mem2d, norm_g, wk, wv)


def _fill_history(ext_ref, src_ref, pad, chunk):
    t = src_ref.shape[1]
    ext_ref[pad:] = src_ref[0]
    for e in range(pad):
        back = e // t + 1
        v = src_ref[0, t - 1 - e % t]
        ext_ref[pad - 1 - e] = jnp.where(chunk >= back, pltpu.roll(v, back, 0), 0.0)


def _pool_conv_tile(uext_ref, hext_ref, p_ref, c_ref, pw_ref, ps_ref, cw_ref, cb_ref, lg_ref, lb_ref,
                    step0, ts, t):
    nc = uext_ref.shape[1]
    chunk = lax.broadcasted_iota(jnp.int32, (nc, 1), 0)
    low = lax.broadcasted_iota(jnp.int32, (1, LANES), 1) < POOL_GROUP
    first_tap = CONV_PAD - (CONV_WIDTH - 1)
    groups_per_tile = LANES // POOL_GROUP

    def one_step(s, carry):
        i = step0 + s
        rows = pl.ds(pl.multiple_of(s * nc, nc), nc)
        pos = (chunk * t + i + 1).astype(F32)
        for lt in range(D_POOL // LANES):
            lanes = slice(lt * LANES, (lt + 1) * LANES)
            w_lo, w_hi = POOL_WINDOWS[groups_per_tile * lt:groups_per_tile * (lt + 1)]
            cur = uext_ref[POOL_PAD + i, :, lanes]
            run = cur
            saved = {}
            for d in range(1, w_hi):
                run = run + uext_ref[POOL_PAD + i - d, :, lanes]
                if d + 1 in (w_lo, w_hi):
                    saved[d + 1] = run
            total = jnp.where(low, saved[w_lo], saved[w_hi])
            count = jnp.minimum(pos, jnp.where(low, float(w_lo), float(w_hi)))
            p_ref[rows, lanes] = total / count - cur
        for lt in range(D_CONV // LANES):
            lanes = slice(lt * LANES, (lt + 1) * LANES)
            acc = jnp.zeros((nc, LANES), F32) + cb_ref[:, lanes]
            for k in range(CONV_WIDTH):
                acc = acc + cw_ref[k:k + 1, lanes] * hext_ref[i + first_tap + k, :, lanes]
            c_ref[rows, lanes] = acc
        return carry

    lax.fori_loop(0, ts, one_step, 0)
    y_pool = jnp.dot(p_ref[...].astype(BF16), pw_ref[...], preferred_element_type=F32) * ps_ref[...]
    conv = c_ref[...]
    mu = jnp.mean(conv, axis=-1, keepdims=True)
    cen = conv - mu
    var = jnp.mean(cen * cen, axis=-1, keepdims=True)
    hn = cen * lax.rsqrt(var + EPS) * lg_ref[...] + lb_ref[...]
    return jnp.concatenate([y_pool, hn * jax.nn.sigmoid(hn)], axis=1).astype(BF16)


def _mix_attn_kernel(x_ref, y4_ref, u4_ref, up_ref, hc_ref, k_ref, v_ref,
                     pw_ref, ps_ref, cw_ref, cb_ref, lg_ref, lb_ref,
                     d_ref, wglut_ref, wo1_ref, wo2_ref, ga_ref, wq_ref, wo_ref,
                     o_ref, uext_ref, hext_ref, p_ref, c_ref):
    _, ts, nc, _ = x_ref.shape
    t = up_ref.shape[1]
    j = pl.program_id(1)
    chunk = lax.broadcasted_iota(jnp.int32, (nc, 1), 0)

    @pl.when(j == 0)
    def _():
        _fill_history(uext_ref, up_ref, POOL_PAD, chunk)
        _fill_history(hext_ref, hc_ref, CONV_PAD, chunk)

    y_pc = _pool_conv_tile(uext_ref, hext_ref, p_ref, c_ref, pw_ref, ps_ref, cw_ref, cb_ref, lg_ref, lb_ref,
                           j * ts, ts, t)

    yt = jnp.concatenate([y4_ref[:, s].reshape(D_SSM, nc) for s in range(ts)], axis=1)
    ut = jnp.concatenate([u4_ref[:, s].reshape(D_SSM, nc) for s in range(ts)], axis=1)
    yt = jax.nn.gelu(yt + d_ref[...] * ut, approximate=True)
    gate = jax.nn.sigmoid(jnp.dot(wglut_ref[...], yt.astype(BF16), preferred_element_type=F32))
    y_ssm = (yt * gate).T.astype(BF16)

    x = x_ref[0].reshape(ts * nc, D_MODEL)
    x = x + jnp.dot(y_ssm, wo1_ref[...], preferred_element_type=F32) \
        + jnp.dot(y_pc, wo2_ref[...], preferred_element_type=F32)

    h = _rms(x, ga_ref[...]).astype(BF16)
    q = jnp.dot(h, wq_ref[...], preferred_element_type=F32) * (XHEAD_DIM ** -0.5)
    heads = []
    for hd in range(N_XHEADS):
        sl = slice(hd * XHEAD_DIM, (hd + 1) * XHEAD_DIM)
        sc = lax.dot_general(q[:, sl].astype(BF16), k_ref[0, :, sl], (((1,), (1,)), ((), ())),
                             preferred_element_type=F32)
        pr = jnp.exp(sc - jnp.max(sc, axis=-1, keepdims=True))
        l = jnp.sum(pr, axis=-1, keepdims=True)
        heads.append(jnp.dot(pr.astype(BF16), v_ref[0, :, sl], preferred_element_type=F32) / l)
    o = jnp.concatenate(heads, axis=-1).astype(BF16)
    x = x + jnp.dot(o, wo_ref[...], preferred_element_type=F32)
    o_ref[0] = x.reshape(ts, nc, D_MODEL)


def _mix_attn(x4, y4, u4, up4, hc4, k3d, v3d, pool_w_bd, pool_scale, conv_w, conv_b, ln_g, ln_b,
              d_col, w_glu_t, w_out1, w_out2, ga, wq, wo, *, layer, ts):
    b, t, nc, _ = x4.shape
    xspec = pl.BlockSpec((1, ts, nc, D_MODEL), lambda i, j: (i, j, 0, 0))
    chan = pl.BlockSpec((N_SSM_GROUPS, ts, SSM_GROUP, nc), lambda i, j: (0, j, 0, i))
    whole = lambda w: pl.BlockSpec((1, t, nc, w), lambda i, j: (i, 0, 0, 0))
    mem = pl.BlockSpec((1, MEM_LEN, D_MODEL), lambda i, j: (i, 0, 0))
    return pl.pallas_call(
        _mix_attn_kernel,
        out_shape=jax.ShapeDtypeStruct((b, t, nc, D_MODEL), F32),
        grid=(b, t // ts),
        in_specs=[
            xspec, chan, chan, whole(D_POOL), whole(D_CONV), mem, mem,
            _const_spec((D_POOL, D_POOL)), _const_spec((1, D_POOL)),
            _const_spec((CONV_WIDTH, D_CONV)), _const_spec((1, D_CONV)),
            _const_spec((1, D_CONV)), _const_spec((1, D_CONV)),
            _const_spec((D_SSM, 1)), _const_spec((D_SSM, D_SSM)),
            _const_spec((D_SSM, D_MODEL)), _const_spec((D_PC, D_MODEL)),
            _const_spec((1, D_MODEL)), _layer_spec((D_MODEL, D_MODEL), layer),
            _layer_spec((D_MODEL, D_MODEL), layer),
        ],
        out_specs=xspec,
        scratch_shapes=[pltpu.VMEM((POOL_PAD + t, nc, D_POOL), F32),
                        pltpu.VMEM((CONV_PAD + t, nc, D_CONV), F32),
                        pltpu.VMEM((ts * nc, D_POOL), F32),
                        pltpu.VMEM((ts * nc, D_CONV), F32)],
        compiler_params=_cparams("parallel", "arbitrary"),
    )(x4, y4, u4, up4, hc4, k3d, v3d, pool_w_bd, pool_scale, conv_w, conv_b, ln_g, ln_b,
      d_col, w_glu_t, w_out1, w_out2, ga, wq, wo)


def _ffn_kernel(x_ref, g_ref, wg_ref, wu_ref, wd_ref, fg_ref, o_ref, *, final):
    y = _ffn_body(x_ref[...], g_ref, wg_ref, wu_ref, wd_ref)
    if final:
        y = _rms(y, fg_ref[...])
    o_ref[...] = y


def _ffn(x2d, norm_g, wg, wu, wd, final_g, *, layer, final, tm):
    n = x2d.shape[0]
    row = pl.BlockSpec((tm, D_MODEL), lambda i: (i, 0))
    return pl.pallas_call(
        functools.partial(_ffn_kernel, final=final),
        out_shape=jax.ShapeDtypeStruct((n, D_MODEL), F32),
        grid=(n // tm,),
        in_specs=[row, _const_spec((1, D_MODEL)), _layer_spec((D_MODEL, D_FF), layer),
                  _layer_spec((D_MODEL, D_FF), layer), _layer_spec((D_FF, D_MODEL), layer),
                  _const_spec((1, D_MODEL))],
        out_specs=row,
        compiler_params=_cparams("parallel"),
    )(x2d, norm_g, wg, wu, wd, final_g)


def _ffn_out_kernel(x_ref, g_ref, wg_ref, wu_ref, wd_ref, fg_ref, o_ref):
    _, ts, nc, _ = x_ref.shape
    y = _ffn_body(x_ref[0].reshape(ts * nc, D_MODEL), g_ref, wg_ref, wu_ref, wd_ref)
    y = _rms(y, fg_ref[...]).reshape(ts, nc, D_MODEL)
    o_ref[0] = jnp.swapaxes(y, 0, 1)


def _ffn_out(x4, norm_g, wg, wu, wd, final_g, *, layer, ts):
    b, t, nc, _ = x4.shape
    return pl.pallas_call(
        _ffn_out_kernel,
        out_shape=jax.ShapeDtypeStruct((b, nc, t, D_MODEL), F32),
        grid=(b, t // ts),
        in_specs=[pl.BlockSpec((1, ts, nc, D_MODEL), lambda i, j: (i, j, 0, 0)),
                  _const_spec((1, D_MODEL)), _layer_spec((D_MODEL, D_FF), layer),
                  _layer_spec((D_MODEL, D_FF), layer), _layer_spec((D_FF, D_MODEL), layer),
                  _const_spec((1, D_MODEL))],
        out_specs=pl.BlockSpec((1, nc, ts, D_MODEL), lambda i, j: (i, 0, j, 0)),
        compiler_params=_cparams("parallel", "parallel"),
    )(x4, norm_g, wg, wu, wd, final_g)


def _tile(n, pref):
    t = max(1, min(n, pref))
    assert n % t == 0, (n, t)
    return t


def kernel(x, mem, ffn1_norm, ffn1_w_gate, ffn1_w_up, ffn1_w_down, mix_norm, w_in, w_out, ssm_lambda_re, ssm_lambda_im, ssm_log_dt, ssm_b_re, ssm_b_im, ssm_c_re, ssm_c_im, ssm_d, ssm_w_glu, pool_w, pool_scale, conv_w, conv_b, conv_ln_g, conv_ln_b, xattn_norm, mem_norm, xattn_wq, xattn_wk, xattn_wv, xattn_wo, ffn2_norm, ffn2_w_gate, ffn2_w_up, ffn2_w_down, final_norm):
    bsz, seq, _ = x.shape
    depth = w_in.shape[0]
    n = bsz * seq
    assert seq % S5_CHUNK == 0
    n_chunks = seq // S5_CHUNK
    assert n_chunks & (n_chunks - 1) == 0 and n_chunks <= (1 << S5_MAX_SCAN_STEPS)
    bg = _tile(bsz, TOKEN_TILE // n_chunks)
    ts = _tile(S5_CHUNK, TOKEN_TILE // n_chunks)
    to = _tile(S5_CHUNK, SUBLANES)
    tm = _tile(n, TOKEN_TILE)
    tkv = _tile(bsz * MEM_LEN, TOKEN_TILE)
    bf = lambda w: w.astype(BF16)
    row = lambda v: v.reshape(1, -1)

    mem2d = mem.reshape(bsz * MEM_LEN, D_MODEL)
    f1g, f1u, f1d = bf(ffn1_w_gate), bf(ffn1_w_up), bf(ffn1_w_down)
    f2g, f2u, f2d = bf(ffn2_w_gate), bf(ffn2_w_up), bf(ffn2_w_down)
    wq, wk, wv, wo = bf(xattn_wq), bf(xattn_wk), bf(xattn_wv), bf(xattn_wo)
    for l in range(depth):
        in_args = (row(ffn1_norm[l]), f1g, f1u, f1d, row(mix_norm[l]), bf(w_in[l, :, :D_SSM].T),
                   bf(w_in[l, :, D_SSM:]))
        if l == 0:
            x4, u4, up4, hc4 = _ffn_mix_in_tm(x.reshape(bsz, n_chunks, S5_CHUNK, D_MODEL), *in_args,
                                              layer=l, ts=to)
        else:
            x4, u4, up4, hc4 = _ffn_mix_in(x4, *in_args, layer=l, bg=bg)

        zt, wst_t, gt, cp = _s5_prep(ssm_lambda_re[l], ssm_lambda_im[l], ssm_log_dt[l], ssm_b_re[l],
                                     ssm_b_im[l], ssm_c_re[l], ssm_c_im[l])
        y4 = _s5_main(u4, jnp.swapaxes(zt, 1, 2), jnp.swapaxes(wst_t, 1, 2), jnp.swapaxes(gt, 1, 2), cp,
                      n_chunks=n_chunks)

        k2d, v2d = _kv(mem2d, row(mem_norm[l]), wk, wv, layer=l, tm=tkv)
        pool_bd = jax.scipy.linalg.block_diag(*[pool_w[l, g] for g in range(len(POOL_WINDOWS))])
        x4 = _mix_attn(x4, y4, u4, up4, hc4, k2d.reshape(bsz, MEM_LEN, D_MODEL),
                       v2d.reshape(bsz, MEM_LEN, D_MODEL), bf(pool_bd), row(pool_scale[l]), conv_w[l],
                       row(conv_b[l]), row(conv_ln_g[l]), row(conv_ln_b[l]), ssm_d[l].reshape(D_SSM, 1),
                       bf(ssm_w_glu[l].T), bf(w_out[l, :D_SSM]), bf(w_out[l, D_SSM:]),
                       row(xattn_norm[l]), wq, wo, layer=l, ts=ts)

        if l == depth - 1:
            out = _ffn_out(x4, row(ffn2_norm[l]), f2g, f2u, f2d, row(final_norm), layer=l, ts=to)
            return out.reshape(bsz, seq, D_MODEL)
        x2d = _ffn(x4.reshape(n, D_MODEL), row(ffn2_norm[l]), f2g, f2u, f2d, row(final_norm),
                   layer=l, final=False, tm=tm)
        x4 = x2d.reshape(bsz, S5_CHUNK, n_chunks, D_MODEL)
```

```python
import functools

import jax
import jax.numpy as jnp
from jax import lax
from jax.experimental import pallas as pl
from jax.experimental.pallas import tpu as pltpu

F32 = jnp.float32
BF16 = jnp.bfloat16

D_MODEL = 1024
MEM_LEN = 256
D_SSM = 384
D_POOL = 256
D_CONV = 384
D_PC = D_POOL + D_CONV
SSM_GROUP = 16
N_SSM_GROUPS = 24
SSM_STATE = 64
POOL_WINDOWS = (2, 4, 8, 16)
POOL_GROUP = 64
CONV_WIDTH = 31
D_IN = D_SSM + D_POOL + 2 * D_CONV
D_FF = 2816
N_XHEADS = 4
XHEAD_DIM = 256
EPS = 1e-6

SUBLANES = 8
LANES = 128
MXU_TILE = 256

TOKEN_TILE = 512
FFN_TOKEN_TILE = 1024
FF_CHUNK = MXU_TILE
S5_SUB = MXU_TILE // SSM_GROUP
S5_CHUNK = S5_SUB
S5_NB = S5_CHUNK // S5_SUB
S5_ROWS = S5_CHUNK * SSM_GROUP
S5_MAX_SCAN_STEPS = 7
CONV_PAD = 32
POOL_PAD = 16
VMEM_LIMIT = 56 * 1024 * 1024


def _cparams(*sem):
    return pltpu.CompilerParams(dimension_semantics=sem, vmem_limit_bytes=VMEM_LIMIT)


def _const_spec(shape):
    nd = len(shape)
    return pl.BlockSpec(shape, lambda *_: (0,) * nd, pipeline_mode=pl.Buffered(1))


def _layer_spec(shape, layer):
    nd = len(shape)
    return pl.BlockSpec((None,) + shape, lambda *_: (layer,) + (0,) * nd, pipeline_mode=pl.Buffered(1))


def _rms(x, g):
    ms = jnp.mean(x * x, axis=-1, keepdims=True)
    return x * lax.rsqrt(ms + EPS) * g


def _ffn_body(x, g_ref, wg_ref, wu_ref, wd_ref):
    h = _rms(x, g_ref[...]).astype(BF16)
    acc = jnp.zeros(x.shape, F32)
    for c in range(D_FF // FF_CHUNK):
        sl = slice(c * FF_CHUNK, (c + 1) * FF_CHUNK)
        gate = jnp.dot(h, wg_ref[:, sl], preferred_element_type=F32)
        up = jnp.dot(h, wu_ref[:, sl], preferred_element_type=F32)
        act = (gate * jax.nn.sigmoid(gate) * up).astype(BF16)
        acc = acc + jnp.dot(act, wd_ref[sl, :], preferred_element_type=F32)
    return x + 0.5 * acc


def _ffn_mix_in_kernel(x_ref, g1_ref, wg_ref, wu_ref, wd_ref, g2_ref, wst_ref, wrest_ref,
                       xo_ref, u4_ref, up_ref, hc_ref):
    b, nc, _ = x_ref.shape
    y = _ffn_body(x_ref[...].reshape(b * nc, D_MODEL), g1_ref, wg_ref, wu_ref, wd_ref)
    xo_ref[...] = y.reshape(b, nc, D_MODEL)
    h = _rms(y, g2_ref[...]).astype(BF16)
    ut = lax.dot_general(wst_ref[...], h, (((1,), (1,)), ((), ())), preferred_element_type=F32)
    u4_ref[...] = ut.reshape(N_SSM_GROUPS, SSM_GROUP, b * nc)
    z = jnp.dot(h, wrest_ref[...], preferred_element_type=F32)
    up_ref[...] = z[:, :D_POOL].reshape(b, nc, D_POOL)
    v = z[:, D_POOL:D_POOL + D_CONV]
    g = z[:, D_POOL + D_CONV:]
    hc_ref[...] = (v * jax.nn.sigmoid(g)).reshape(b, nc, D_CONV)


def _ffn_mix_in_tm_kernel(x_ref, g1_ref, wg_ref, wu_ref, wd_ref, g2_ref, wst_ref, wrest_ref,
                          xo_ref, u4_ref, up_ref, hc_ref):
    _, nc, ts, _ = x_ref.shape
    x = jnp.swapaxes(x_ref[0], 0, 1).reshape(ts * nc, D_MODEL)
    y = _ffn_body(x, g1_ref, wg_ref, wu_ref, wd_ref)
    xo_ref[0] = y.reshape(ts, nc, D_MODEL)
    h = _rms(y, g2_ref[...]).astype(BF16)
    ut = lax.dot_general(wst_ref[...], h, (((1,), (1,)), ((), ())), preferred_element_type=F32)
    for s in range(ts):
        u4_ref[:, s] = ut[:, s * nc:(s + 1) * nc].reshape(N_SSM_GROUPS, SSM_GROUP, nc)
    z = jnp.dot(h, wrest_ref[...], preferred_element_type=F32)
    up_ref[0] = z[:, :D_POOL].reshape(ts, nc, D_POOL)
    v = z[:, D_POOL:D_POOL + D_CONV]
    g = z[:, D_POOL + D_CONV:]
    hc_ref[0] = (v * jax.nn.sigmoid(g)).reshape(ts, nc, D_CONV)


def _ffn_mix_in_tm(x_tm, g1, wg, wu, wd, g2, w_ssm_t, w_rest, *, layer, ts):
    b, nc, t, _ = x_tm.shape
    tile = lambda w: pl.BlockSpec((1, ts, nc, w), lambda i, j: (i, j, 0, 0))
    return pl.pallas_call(
        _ffn_mix_in_tm_kernel,
        out_shape=(jax.ShapeDtypeStruct((b, t, nc, D_MODEL), F32),
                   jax.ShapeDtypeStruct((N_SSM_GROUPS, t, SSM_GROUP, b * nc), F32),
                   jax.ShapeDtypeStruct((b, t, nc, D_POOL), F32),
                   jax.ShapeDtypeStruct((b, t, nc, D_CONV), F32)),
        grid=(b, t // ts),
        in_specs=[
            pl.BlockSpec((1, nc, ts, D_MODEL), lambda i, j: (i, 0, j, 0)),
            _const_spec((1, D_MODEL)),
            _layer_spec((D_MODEL, D_FF), layer),
            _layer_spec((D_MODEL, D_FF), layer),
            _layer_spec((D_FF, D_MODEL), layer),
            _const_spec((1, D_MODEL)),
            _const_spec((D_SSM, D_MODEL)),
            _const_spec((D_MODEL, D_IN - D_SSM)),
        ],
        out_specs=(tile(D_MODEL),
                   pl.BlockSpec((N_SSM_GROUPS, ts, SSM_GROUP, nc), lambda i, j: (0, j, 0, i)),
                   tile(D_POOL), tile(D_CONV)),
        compiler_params=_cparams("parallel", "parallel"),
    )(x_tm, g1, wg, wu, wd, g2, w_ssm_t, w_rest)


def _ffn_mix_in(x4, g1, wg, wu, wd, g2, w_ssm_t, w_rest, *, layer, bg):
    b, t, nc, _ = x4.shape
    step = lambda w: pl.BlockSpec((bg, None, nc, w), lambda g, i: (g, i, 0, 0))
    return pl.pallas_call(
        _ffn_mix_in_kernel,
        out_shape=(jax.ShapeDtypeStruct((b, t, nc, D_MODEL), F32),
                   jax.ShapeDtypeStruct((N_SSM_GROUPS, t, SSM_GROUP, b * nc), F32),
                   jax.ShapeDtypeStruct((b, t, nc, D_POOL), F32),
                   jax.ShapeDtypeStruct((b, t, nc, D_CONV), F32)),
        grid=(b // bg, t),
        in_specs=[
            step(D_MODEL),
            _const_spec((1, D_MODEL)),
            _layer_spec((D_MODEL, D_FF), layer),
            _layer_spec((D_MODEL, D_FF), layer),
            _layer_spec((D_FF, D_MODEL), layer),
            _const_spec((1, D_MODEL)),
            _const_spec((D_SSM, D_MODEL)),
            _const_spec((D_MODEL, D_IN - D_SSM)),
        ],
        out_specs=(step(D_MODEL),
                   pl.BlockSpec((N_SSM_GROUPS, None, SSM_GROUP, bg * nc), lambda g, i: (0, i, 0, g)),
                   step(D_POOL), step(D_CONV)),
        compiler_params=_cparams("parallel", "parallel"),
    )(x4, g1, wg, wu, wd, g2, w_ssm_t, w_rest)


def _s5_prep_kernel(lr2_ref, li2_ref, ldt_ref, br2_ref, bi2_ref, lrc_ref, lic_ref, crt_ref, cit_ref,
                    zt_ref, wst_ref, gt_ref, cp_ref):
    hi = lax.Precision.HIGHEST
    lane = lax.broadcasted_iota(jnp.int32, (1, LANES), 1)
    first = lane < SSM_STATE

    lr2 = lr2_ref[0]
    li2 = li2_ref[0]
    dt = jnp.exp(ldt_ref[0])
    rho = lr2 * dt
    th = li2 * dt

    def pw_row(e):
        mag = jnp.exp(e * rho)
        return mag * jnp.cos(e * th), mag * jnp.sin(e * th)

    one = jnp.ones((1, 1), F32)
    a_r, a_i = pw_row(one)
    den = lr2 * lr2 + li2 * li2
    z_r = ((a_r - 1.0) * lr2 + a_i * li2) / den
    z_i = (a_i * lr2 - (a_r - 1.0) * li2) / den
    br2 = br2_ref[0]
    bi2 = bi2_ref[0]
    b1 = jnp.where(first, br2, bi2)
    b2 = jnp.where(first, -bi2, br2)
    y1 = z_r * b1 + z_i * b2
    y2 = z_r * b2 - z_i * b1

    e_lo = (S5_SUB - 1 - lax.broadcasted_iota(jnp.int32, (S5_SUB, 1), 0)).astype(F32)
    lo_r, lo_i = pw_row(e_lo)
    lob1 = jnp.concatenate([lo_r[i:i + 1] * y1 + lo_i[i:i + 1] * y2 for i in range(S5_SUB)], axis=0)
    lob2 = jnp.concatenate([lo_r[i:i + 1] * y2 - lo_i[i:i + 1] * y1 for i in range(S5_SUB)], axis=0)
    e_hi = (S5_SUB * (S5_NB - 1 - lax.broadcasted_iota(jnp.int32, (S5_NB, 1), 0))).astype(F32)
    hi_r, hi_i = pw_row(e_hi)
    pb = jnp.concatenate([hi_r[m:m + 1] * lob1 + hi_i[m:m + 1] * lob2 for m in range(S5_NB)], axis=0)
    wst_ref[0] = pb.astype(BF16)

    lrc = lrc_ref[0]
    lic = lic_ref[0]
    rho_c = lrc * dt
    th_c = lic * dt
    mag_c = jnp.exp(rho_c)
    ac_r = mag_c * jnp.cos(th_c)
    ac_i = mag_c * jnp.sin(th_c)
    rc_r, rc_i = [], []
    for half in range(MXU_TILE // LANES):
        e = (lane // SSM_GROUP + half * (LANES // SSM_GROUP)).astype(F32)
        mag = jnp.exp(e * rho_c)
        t_r = mag * jnp.cos(e * th_c)
        t_i = mag * jnp.sin(e * th_c)
        c_r = crt_ref[0, :, half * LANES:(half + 1) * LANES]
        c_i = cit_ref[0, :, half * LANES:(half + 1) * LANES]
        rc_r.append(t_r * c_r - t_i * c_i)
        rc_i.append(t_r * c_i + t_i * c_r)
    ct_stack = jnp.concatenate([crt_ref[0], -cit_ref[0]], axis=0)

    def csq(v_r, v_i):
        return v_r * v_r - v_i * v_i, 2.0 * v_r * v_i

    a16_r, a16_i = ac_r, ac_i
    for _ in range(S5_SUB.bit_length() - 1):
        a16_r, a16_i = csq(a16_r, a16_i)
    g_r, g_i = ac_r, ac_i
    for j in range(S5_NB):
        for half in range(MXU_TILE // LANES):
            lo = j * MXU_TILE + half * LANES
            gt_ref[0, :SSM_STATE, lo:lo + LANES] = (g_r * rc_r[half] - g_i * rc_i[half]).astype(BF16)
            gt_ref[0, SSM_STATE:, lo:lo + LANES] = (-(g_r * rc_i[half] + g_i * rc_r[half])).astype(BF16)
        g_r, g_i = g_r * a16_r - g_i * a16_i, g_r * a16_i + g_i * a16_r

    w_r, w_i = a16_r, a16_i
    for _ in range(S5_NB.bit_length() - 1):
        w_r, w_i = csq(w_r, w_i)
    for r in range(S5_MAX_SCAN_STEPS):
        cp_ref[0, 2 * r * SSM_STATE:(2 * r + 1) * SSM_STATE, :] = w_r
        cp_ref[0, (2 * r + 1) * SSM_STATE:(2 * r + 2) * SSM_STATE, :] = w_i
        w_r, w_i = csq(w_r, w_i)

    off_rows = (S5_CHUNK - S5_SUB) * SSM_GROUP
    if off_rows:
        rc_stack = jnp.concatenate([jnp.concatenate(rc_r, axis=1), -jnp.concatenate(rc_i, axis=1)], axis=0)
        lo_row = (S5_SUB - 1) * SSM_GROUP
        zt_off = jnp.dot(pb[lo_row:lo_row + off_rows], rc_stack, precision=hi, preferred_element_type=F32)
        zt_ref[0, :off_rows, :] = zt_off.astype(BF16)
    kall = jnp.dot(pb[off_rows:], ct_stack, precision=hi, preferred_element_type=F32)
    colblk = lax.broadcasted_iota(jnp.int32, (1, MXU_TILE), 1) // SSM_GROUP
    zd = jnp.zeros((MXU_TILE, MXU_TILE), F32)
    for jj in range(S5_SUB):
        sh = (S5_SUB - 1 - jj) * SSM_GROUP
        shifted = kall if sh == 0 else jnp.concatenate(
            [kall[sh:], jnp.zeros((sh, MXU_TILE), F32)], axis=0)
        zd = jnp.where(colblk == jj, shifted, zd)
    zt_ref[0, off_rows:, :] = zd.astype(BF16)


def _s5_prep(lam_re, lam_im, log_dt, b_re, b_im, c_re, c_im):
    g = N_SSM_GROUPS
    lr2 = jnp.tile(lam_re, (1, 2))[:, None, :]
    li2 = jnp.tile(lam_im, (1, 2))[:, None, :]
    ldt = jnp.broadcast_to(log_dt[:, None, None], (g, 1, LANES))
    br2 = jnp.tile(jnp.swapaxes(b_re, 1, 2), (1, 1, 2))
    bi2 = jnp.tile(jnp.swapaxes(b_im, 1, 2), (1, 1, 2))
    lrc = jnp.broadcast_to(lam_re[:, :, None], (g, SSM_STATE, LANES))
    lic = jnp.broadcast_to(lam_im[:, :, None], (g, SSM_STATE, LANES))
    crt = jnp.tile(jnp.swapaxes(c_re, 1, 2), (1, 1, S5_SUB))
    cit = jnp.tile(jnp.swapaxes(c_im, 1, 2), (1, 1, S5_SUB))

    def spec(*shape):
        return pl.BlockSpec((1,) + shape, lambda i: (i, 0, 0))

    cp_rows = 2 * S5_MAX_SCAN_STEPS * SSM_STATE
    return pl.pallas_call(
        _s5_prep_kernel,
        out_shape=(jax.ShapeDtypeStruct((g, S5_ROWS, MXU_TILE), BF16),
                   jax.ShapeDtypeStruct((g, S5_ROWS, LANES), BF16),
                   jax.ShapeDtypeStruct((g, LANES, S5_ROWS), BF16),
                   jax.ShapeDtypeStruct((g, cp_rows, LANES), F32)),
        grid=(g,),
        in_specs=[spec(1, LANES), spec(1, LANES), spec(1, LANES),
                  spec(SSM_GROUP, LANES), spec(SSM_GROUP, LANES),
                  spec(SSM_STATE, LANES), spec(SSM_STATE, LANES),
                  spec(SSM_STATE, MXU_TILE), spec(SSM_STATE, MXU_TILE)],
        out_specs=(spec(S5_ROWS, MXU_TILE), spec(S5_ROWS, LANES), spec(LANES, S5_ROWS),
                   spec(cp_rows, LANES)),
        compiler_params=_cparams("parallel"),
    )(lr2, li2, ldt, br2, bi2, lrc, lic, crt, cit)


def _s5_main_kernel(u_ref, zw_ref, wst_ref, g_ref, cp_ref, o_ref, *, n_chunks):
    ncols = u_ref.shape[-1]
    u = u_ref[0].reshape(S5_ROWS, ncols).astype(BF16)
    state = jnp.dot(wst_ref[0], u, preferred_element_type=F32)
    s_r, s_i = state[:SSM_STATE], state[SSM_STATE:]
    chunk = lax.broadcasted_iota(jnp.int32, (1, ncols), 1) % n_chunks
    reps = ncols // LANES if ncols > LANES else 1

    def wide(v):
        return jnp.concatenate([v] * reps, axis=1)[:, :ncols]

    r = 0
    while (1 << r) < n_chunks:
        s = 1 << r
        p_r = jnp.where(chunk >= s, pltpu.roll(s_r, s, 1), 0.0)
        p_i = jnp.where(chunk >= s, pltpu.roll(s_i, s, 1), 0.0)
        w_r = wide(cp_ref[0, 2 * r * SSM_STATE:(2 * r + 1) * SSM_STATE, :])
        w_i = wide(cp_ref[0, (2 * r + 1) * SSM_STATE:(2 * r + 2) * SSM_STATE, :])
        s_r, s_i = s_r + w_r * p_r - w_i * p_i, s_i + w_r * p_i + w_i * p_r
        r += 1
    c_r = jnp.where(chunk >= 1, pltpu.roll(s_r, 1, 1), 0.0)
    c_i = jnp.where(chunk >= 1, pltpu.roll(s_i, 1, 1), 0.0)
    carry = jnp.concatenate([c_r, c_i], axis=0).astype(BF16)
    for j in range(S5_NB):
        k = (j + 1) * MXU_TILE
        y = jnp.dot(zw_ref[0, :, (S5_NB - 1 - j) * MXU_TILE:], u[:k], preferred_element_type=F32)
        y = y + jnp.dot(g_ref[0, j * MXU_TILE:(j + 1) * MXU_TILE, :], carry, preferred_element_type=F32)
        o_ref[0, j * S5_SUB:(j + 1) * S5_SUB] = y.reshape(S5_SUB, SSM_GROUP, ncols)


def _s5_main(u4, zw, wst, gm, cp, *, n_chunks):
    g, t, _, ncols = u4.shape
    cp_rows = cp.shape[1]

    def spec(*shape):
        return pl.BlockSpec((1,) + shape, lambda i: (i,) + (0,) * len(shape))

    return pl.pallas_call(
        functools.partial(_s5_main_kernel, n_chunks=n_chunks),
        out_shape=jax.ShapeDtypeStruct((g, t, SSM_GROUP, ncols), F32),
        grid=(g,),
        in_specs=[spec(t, SSM_GROUP, ncols), spec(MXU_TILE, S5_ROWS), spec(LANES, S5_ROWS),
                  spec(S5_ROWS, LANES), spec(cp_rows, LANES)],
        out_specs=spec(t, SSM_GROUP, ncols),
        compiler_params=_cparams("parallel"),
    )(u4, zw, wst, gm, cp)


def _kv_kernel(m_ref, g_ref, wk_ref, wv_ref, k_ref, v_ref):
    m = _rms(m_ref[...], g_ref[...]).astype(BF16)
    k_ref[...] = jnp.dot(m, wk_ref[...], preferred_element_type=F32).astype(BF16)
    v_ref[...] = jnp.dot(m, wv_ref[...], preferred_element_type=F32).astype(BF16)


def _kv(mem2d, norm_g, wk, wv, *, layer, tm):
    n = mem2d.shape[0]
    row = pl.BlockSpec((tm, D_MODEL), lambda i: (i, 0))
    return pl.pallas_call(
        _kv_kernel,
        out_shape=(jax.ShapeDtypeStruct((n, D_MODEL), BF16), jax.ShapeDtypeStruct((n, D_MODEL), BF16)),
        grid=(n // tm,),
        in_specs=[row, _const_spec((1, D_MODEL)), _layer_spec((D_MODEL, D_MODEL), layer),
                  _layer_spec((D_MODEL, D_MODEL), layer)],
        out_specs=(row, row),
        compiler_params=_cparams("parallel"),
    )(mem2d, norm_g, wk, wv)


def _fill_history(ext_ref, src_ref, pad, chunk):
    t = src_ref.shape[1]
    ext_ref[pad:] = src_ref[0]
    for e in range(pad):
        back = e // t + 1
        v = src_ref[0, t - 1 - e % t]
        ext_ref[pad - 1 - e] = jnp.where(chunk >= back, pltpu.roll(v, back, 0), 0.0)


def _pool_conv_tile(uext_ref, hext_ref, p_ref, c_ref, pw_ref, ps_ref, cw_ref, cb_ref, lg_ref, lb_ref,
                    step0, ts, t):
    nc = uext_ref.shape[1]
    chunk = lax.broadcasted_iota(jnp.int32, (nc, 1), 0)
    low = lax.broadcasted_iota(jnp.int32, (1, LANES), 1) < POOL_GROUP
    first_tap = CONV_PAD - (CONV_WIDTH - 1)
    groups_per_tile = LANES // POOL_GROUP

    def one_step(s, carry):
        i = step0 + s
        rows = pl.ds(pl.multiple_of(s * nc, nc), nc)
        pos = (chunk * t + i + 1).astype(F32)
        for lt in range(D_POOL // LANES):
            lanes = slice(lt * LANES, (lt + 1) * LANES)
            w_lo, w_hi = POOL_WINDOWS[groups_per_tile * lt:groups_per_tile * (lt + 1)]
            cur = uext_ref[POOL_PAD + i, :, lanes]
            run = cur
            saved = {}
            for d in range(1, w_hi):
                run = run + uext_ref[POOL_PAD + i - d, :, lanes]
                if d + 1 in (w_lo, w_hi):
                    saved[d + 1] = run
            total = jnp.where(low, saved[w_lo], saved[w_hi])
            count = jnp.minimum(pos, jnp.where(low, float(w_lo), float(w_hi)))
            p_ref[rows, lanes] = total / count - cur
        for lt in range(D_CONV // LANES):
            lanes = slice(lt * LANES, (lt + 1) * LANES)
            acc = jnp.zeros((nc, LANES), F32) + cb_ref[:, lanes]
            for k in range(CONV_WIDTH):
                acc = acc + cw_ref[k:k + 1, lanes] * hext_ref[i + first_tap + k, :, lanes]
            c_ref[rows, lanes] = acc
        return carry

    lax.fori_loop(0, ts, one_step, 0)
    y_pool = jnp.dot(p_ref[...].astype(BF16), pw_ref[...], preferred_element_type=F32) * ps_ref[...]
    conv = c_ref[...]
    mu = jnp.mean(conv, axis=-1, keepdims=True)
    cen = conv - mu
    var = jnp.mean(cen * cen, axis=-1, keepdims=True)
    hn = cen * lax.rsqrt(var + EPS) * lg_ref[...] + lb_ref[...]
    return jnp.concatenate([y_pool, hn * jax.nn.sigmoid(hn)], axis=1).astype(BF16)


def _mix_attn_kernel(x_ref, y4_ref, u4_ref, up_ref, hc_ref, k_ref, v_ref,
                     pw_ref, ps_ref, cw_ref, cb_ref, lg_ref, lb_ref,
                     d_ref, wglut_ref, wo1_ref, wo2_ref, ga_ref, wq_ref, wo_ref,
                     o_ref, uext_ref, hext_ref, p_ref, c_ref):
    _, ts, nc, _ = x_ref.shape
    t = up_ref.shape[1]
    j = pl.program_id(1)
    chunk = lax.broadcasted_iota(jnp.int32, (nc, 1), 0)

    @pl.when(j == 0)
    def _():
        _fill_history(uext_ref, up_ref, POOL_PAD, chunk)
        _fill_history(hext_ref, hc_ref, CONV_PAD, chunk)

    y_pc = _pool_conv_tile(uext_ref, hext_ref, p_ref, c_ref, pw_ref, ps_ref, cw_ref, cb_ref, lg_ref, lb_ref,
                           j * ts, ts, t)

    yt = jnp.concatenate([y4_ref[:, s].reshape(D_SSM, nc) for s in range(ts)], axis=1)
    ut = jnp.concatenate([u4_ref[:, s].reshape(D_SSM, nc) for s in range(ts)], axis=1)
    yt = jax.nn.gelu(yt + d_ref[...] * ut, approximate=True)
    gate = jax.nn.sigmoid(jnp.dot(wglut_ref[...], yt.astype(BF16), preferred_element_type=F32))
    y_ssm = (yt * gate).T.astype(BF16)

    x = x_ref[0].reshape(ts * nc, D_MODEL)
    x = x + jnp.dot(y_ssm, wo1_ref[...], preferred_element_type=F32) \
        + jnp.dot(y_pc, wo2_ref[...], preferred_element_type=F32)

    h = _rms(x, ga_ref[...]).astype(BF16)
    q = jnp.dot(h, wq_ref[...], preferred_element_type=F32) * (XHEAD_DIM ** -0.5)
    heads = []
    for hd in range(N_XHEADS):
        sl = slice(hd * XHEAD_DIM, (hd + 1) * XHEAD_DIM)
        sc = lax.dot_general(q[:, sl].astype(BF16), k_ref[0, :, sl], (((1,), (1,)), ((), ())),
                             preferred_element_type=F32)
        pr = jnp.exp(sc - jnp.max(sc, axis=-1, keepdims=True))
        l = jnp.sum(pr, axis=-1, keepdims=True)
        heads.append(jnp.dot(pr.astype(BF16), v_ref[0, :, sl], preferred_element_type=F32) / l)
    o = jnp.concatenate(heads, axis=-1).astype(BF16)
    x = x + jnp.dot(o, wo_ref[...], preferred_element_type=F32)
    o_ref[0] = x.reshape(ts, nc, D_MODEL)


def _mix_attn(x4, y4, u4, up4, hc4, k3d, v3d, pool_w_bd, pool_scale, conv_w, conv_b, ln_g, ln_b,
              d_col, w_glu_t, w_out1, w_out2, ga, wq, wo, *, layer, ts):
    b, t, nc, _ = x4.shape
    xspec = pl.BlockSpec((1, ts, nc, D_MODEL), lambda i, j: (i, j, 0, 0))
    chan = pl.BlockSpec((N_SSM_GROUPS, ts, SSM_GROUP, nc), lambda i, j: (0, j, 0, i))
    whole = lambda w: pl.BlockSpec((1, t, nc, w), lambda i, j: (i, 0, 0, 0))
    mem = pl.BlockSpec((1, MEM_LEN, D_MODEL), lambda i, j: (i, 0, 0))
    return pl.pallas_call(
        _mix_attn_kernel,
        out_shape=jax.ShapeDtypeStruct((b, t, nc, D_MODEL), F32),
        grid=(b, t // ts),
        in_specs=[
            xspec, chan, chan, whole(D_POOL), whole(D_CONV), mem, mem,
            _const_spec((D_POOL, D_POOL)), _const_spec((1, D_POOL)),
            _const_spec((CONV_WIDTH, D_CONV)), _const_spec((1, D_CONV)),
            _const_spec((1, D_CONV)), _const_spec((1, D_CONV)),
            _const_spec((D_SSM, 1)), _const_spec((D_SSM, D_SSM)),
            _const_spec((D_SSM, D_MODEL)), _const_spec((D_PC, D_MODEL)),
            _const_spec((1, D_MODEL)), _layer_spec((D_MODEL, D_MODEL), layer),
            _layer_spec((D_MODEL, D_MODEL), layer),
        ],
        out_specs=xspec,
        scratch_shapes=[pltpu.VMEM((POOL_PAD + t, nc, D_POOL), F32),
                        pltpu.VMEM((CONV_PAD + t, nc, D_CONV), F32),
                        pltpu.VMEM((ts * nc, D_POOL), F32),
                        pltpu.VMEM((ts * nc, D_CONV), F32)],
        compiler_params=_cparams("parallel", "arbitrary"),
    )(x4, y4, u4, up4, hc4, k3d, v3d, pool_w_bd, pool_scale, conv_w, conv_b, ln_g, ln_b,
      d_col, w_glu_t, w_out1, w_out2, ga, wq, wo)


def _ffn_kernel(x_ref, g_ref, wg_ref, wu_ref, wd_ref, o_ref):
    o_ref[...] = _ffn_body(x_ref[...], g_ref, wg_ref, wu_ref, wd_ref)


def _ffn(x2d, norm_g, wg, wu, wd, *, layer, tm):
    n = x2d.shape[0]
    row = pl.BlockSpec((tm, D_MODEL), lambda i: (i, 0))
    return pl.pallas_call(
        _ffn_kernel,
        out_shape=jax.ShapeDtypeStruct((n, D_MODEL), F32),
        grid=(n // tm,),
        in_specs=[row, _const_spec((1, D_MODEL)), _layer_spec((D_MODEL, D_FF), layer),
                  _layer_spec((D_MODEL, D_FF), layer), _layer_spec((D_FF, D_MODEL), layer)],
        out_specs=row,
        compiler_params=_cparams("parallel"),
    )(x2d, norm_g, wg, wu, wd)


def _ffn_out_kernel(x_ref, g_ref, wg_ref, wu_ref, wd_ref, fg_ref, o_ref):
    _, ts, nc, _ = x_ref.shape
    y = _ffn_body(x_ref[0].reshape(ts * nc, D_MODEL), g_ref, wg_ref, wu_ref, wd_ref)
    y = _rms(y, fg_ref[...]).reshape(ts, nc, D_MODEL)
    o_ref[0] = jnp.swapaxes(y, 0, 1)


def _ffn_out(x4, norm_g, wg, wu, wd, final_g, *, layer, ts):
    b, t, nc, _ = x4.shape
    return pl.pallas_call(
        _ffn_out_kernel,
        out_shape=jax.ShapeDtypeStruct((b, nc, t, D_MODEL), F32),
        grid=(b, t // ts),
        in_specs=[pl.BlockSpec((1, ts, nc, D_MODEL), lambda i, j: (i, j, 0, 0)),
                  _const_spec((1, D_MODEL)), _layer_spec((D_MODEL, D_FF), layer),
                  _layer_spec((D_MODEL, D_FF), layer), _layer_spec((D_FF, D_MODEL), layer),
                  _const_spec((1, D_MODEL))],
        out_specs=pl.BlockSpec((1, nc, ts, D_MODEL), lambda i, j: (i, 0, j, 0)),
        compiler_params=_cparams("parallel", "parallel"),
    )(x4, norm_g, wg, wu, wd, final_g)


def _tile(n, pref):
    t = max(1, min(n, pref))
    assert n % t == 0, (n, t)
    return t


def kernel(x, mem, ffn1_norm, ffn1_w_gate, ffn1_w_up, ffn1_w_down, mix_norm, w_in, w_out, ssm_lambda_re, ssm_lambda_im, ssm_log_dt, ssm_b_re, ssm_b_im, ssm_c_re, ssm_c_im, ssm_d, ssm_w_glu, pool_w, pool_scale, conv_w, conv_b, conv_ln_g, conv_ln_b, xattn_norm, mem_norm, xattn_wq, xattn_wk, xattn_wv, xattn_wo, ffn2_norm, ffn2_w_gate, ffn2_w_up, ffn2_w_down, final_norm):
    bsz, seq, _ = x.shape
    depth = w_in.shape[0]
    n = bsz * seq
    assert seq % S5_CHUNK == 0
    n_chunks = seq // S5_CHUNK
    assert n_chunks & (n_chunks - 1) == 0 and n_chunks <= (1 << S5_MAX_SCAN_STEPS)
    bg = _tile(bsz, FFN_TOKEN_TILE // n_chunks)
    ts = _tile(S5_CHUNK, TOKEN_TILE // n_chunks)
    to = _tile(S5_CHUNK, SUBLANES)
    tm = _tile(n, FFN_TOKEN_TILE)
    tkv = _tile(bsz * MEM_LEN, TOKEN_TILE)
    bf = lambda w: w.astype(BF16)
    row = lambda v: v.reshape(1, -1)

    mem2d = mem.reshape(bsz * MEM_LEN, D_MODEL)
    f1g, f1u, f1d = bf(ffn1_w_gate), bf(ffn1_w_up), bf(ffn1_w_down)
    f2g, f2u, f2d = bf(ffn2_w_gate), bf(ffn2_w_up), bf(ffn2_w_down)
    wq, wk, wv, wo = bf(xattn_wq), bf(xattn_wk), bf(xattn_wv), bf(xattn_wo)
    for l in range(depth):
        in_args = (row(ffn1_norm[l]), f1g, f1u, f1d, row(mix_norm[l]), bf(w_in[l, :, :D_SSM].T),
                   bf(w_in[l, :, D_SSM:]))
        if l == 0:
            x4, u4, up4, hc4 = _ffn_mix_in_tm(x.reshape(bsz, n_chunks, S5_CHUNK, D_MODEL), *in_args,
                                              layer=l, ts=to)
        else:
            x4, u4, up4, hc4 = _ffn_mix_in(x4, *in_args, layer=l, bg=bg)

        zt, wst_t, gt, cp = _s5_prep(ssm_lambda_re[l], ssm_lambda_im[l], ssm_log_dt[l], ssm_b_re[l],
                                     ssm_b_im[l], ssm_c_re[l], ssm_c_im[l])
        y4 = _s5_main(u4, jnp.swapaxes(zt, 1, 2), jnp.swapaxes(wst_t, 1, 2), jnp.swapaxes(gt, 1, 2), cp,
                      n_chunks=n_chunks)

        k2d, v2d = _kv(mem2d, row(mem_norm[l]), wk, wv, layer=l, tm=tkv)
        pool_bd = jax.scipy.linalg.block_diag(*[pool_w[l, g] for g in range(len(POOL_WINDOWS))])
        x4 = _mix_attn(x4, y4, u4, up4, hc4, k2d.reshape(bsz, MEM_LEN, D_MODEL),
                       v2d.reshape(bsz, MEM_LEN, D_MODEL), bf(pool_bd), row(pool_scale[l]), conv_w[l],
                       row(conv_b[l]), row(conv_ln_g[l]), row(conv_ln_b[l]), ssm_d[l].reshape(D_SSM, 1),
                       bf(ssm_w_glu[l].T), bf(w_out[l, :D_SSM]), bf(w_out[l, D_SSM:]),
                       row(xattn_norm[l]), wq, wo, layer=l, ts=ts)

        if l == depth - 1:
            out = _ffn_out(x4, row(ffn2_norm[l]), f2g, f2u, f2d, row(final_norm), layer=l, ts=to)
            return out.reshape(bsz, seq, D_MODEL)
        x2d = _ffn(x4.reshape(n, D_MODEL), row(ffn2_norm[l]), f2g, f2u, f2d, layer=l, tm=tm)
        x4 = x2d.reshape(bsz, S5_CHUNK, n_chunks, D_MODEL)
```

```python
import functools

import jax
import jax.numpy as jnp
from jax import lax
from jax.experimental import pallas as pl
from jax.experimental.pallas import tpu as pltpu

F32 = jnp.float32
BF16 = jnp.bfloat16

D_MODEL = 1024
MEM_LEN = 256
D_SSM = 384
D_POOL = 256
D_CONV = 384
D_PC = D_POOL + D_CONV
SSM_GROUP = 16
N_SSM_GROUPS = 24
SSM_STATE = 64
POOL_WINDOWS = (2, 4, 8, 16)
POOL_GROUP = 64
CONV_WIDTH = 31
D_IN = D_SSM + D_POOL + 2 * D_CONV
D_FF = 2816
N_XHEADS = 4
XHEAD_DIM = 256
EPS = 1e-6

SUBLANES = 8
LANES = 128
MXU_TILE = 256

TOKEN_TILE = 512
FFN_TOKEN_TILE = 1024
FF_CHUNK = MXU_TILE
S5_SUB = MXU_TILE // SSM_GROUP
S5_CHUNK = S5_SUB
S5_NB = S5_CHUNK // S5_SUB
S5_ROWS = S5_CHUNK * SSM_GROUP
S5_MAX_SCAN_STEPS = 7
CONV_PAD = 32
POOL_PAD = 16
CONV_ROWS = 32
VMEM_LIMIT = 56 * 1024 * 1024


def _cparams(*sem):
    return pltpu.CompilerParams(dimension_semantics=sem, vmem_limit_bytes=VMEM_LIMIT)


def _const_spec(shape):
    nd = len(shape)
    return pl.BlockSpec(shape, lambda *_: (0,) * nd, pipeline_mode=pl.Buffered(1))


def _layer_spec(shape, layer):
    nd = len(shape)
    return pl.BlockSpec((None,) + shape, lambda *_: (layer,) + (0,) * nd, pipeline_mode=pl.Buffered(1))


def _rms(x, g):
    ms = jnp.mean(x * x, axis=-1, keepdims=True)
    return x * lax.rsqrt(ms + EPS) * g


def _ffn_body(x, g_ref, wg_ref, wu_ref, wd_ref):
    h = _rms(x, g_ref[...]).astype(BF16)
    acc = jnp.zeros(x.shape, F32)
    for c in range(D_FF // FF_CHUNK):
        sl = slice(c * FF_CHUNK, (c + 1) * FF_CHUNK)
        gate = jnp.dot(h, wg_ref[:, sl], preferred_element_type=F32)
        up = jnp.dot(h, wu_ref[:, sl], preferred_element_type=F32)
        act = (gate * jax.nn.sigmoid(gate) * up).astype(BF16)
        acc = acc + jnp.dot(act, wd_ref[sl, :], preferred_element_type=F32)
    return x + 0.5 * acc


def _ffn_mix_in_kernel(x_ref, g1_ref, wg_ref, wu_ref, wd_ref, g2_ref, wst_ref, wrest_ref,
                       xo_ref, u4_ref, up_ref, hc_ref):
    b, nc, _ = x_ref.shape
    y = _ffn_body(x_ref[...].reshape(b * nc, D_MODEL), g1_ref, wg_ref, wu_ref, wd_ref)
    xo_ref[...] = y.reshape(b, nc, D_MODEL)
    h = _rms(y, g2_ref[...]).astype(BF16)
    ut = lax.dot_general(wst_ref[...], h, (((1,), (1,)), ((), ())), preferred_element_type=F32)
    u4_ref[...] = ut.reshape(N_SSM_GROUPS, SSM_GROUP, b * nc)
    z = jnp.dot(h, wrest_ref[...], preferred_element_type=F32)
    up_ref[...] = z[:, :D_POOL].reshape(b, nc, D_POOL)
    v = z[:, D_POOL:D_POOL + D_CONV]
    g = z[:, D_POOL + D_CONV:]
    hc_ref[...] = (v * jax.nn.sigmoid(g)).reshape(b, nc, D_CONV)


def _ffn_mix_in_tm_kernel(x_ref, g1_ref, wg_ref, wu_ref, wd_ref, g2_ref, wst_ref, wrest_ref,
                          xo_ref, u4_ref, up_ref, hc_ref):
    _, nc, ts, _ = x_ref.shape
    x = jnp.swapaxes(x_ref[0], 0, 1).reshape(ts * nc, D_MODEL)
    y = _ffn_body(x, g1_ref, wg_ref, wu_ref, wd_ref)
    xo_ref[0] = y.reshape(ts, nc, D_MODEL)
    h = _rms(y, g2_ref[...]).astype(BF16)
    ut = lax.dot_general(wst_ref[...], h, (((1,), (1,)), ((), ())), preferred_element_type=F32)
    for s in range(ts):
        u4_ref[:, s] = ut[:, s * nc:(s + 1) * nc].reshape(N_SSM_GROUPS, SSM_GROUP, nc)
    z = jnp.dot(h, wrest_ref[...], preferred_element_type=F32)
    up_ref[0] = z[:, :D_POOL].reshape(ts, nc, D_POOL)
    v = z[:, D_POOL:D_POOL + D_CONV]
    g = z[:, D_POOL + D_CONV:]
    hc_ref[0] = (v * jax.nn.sigmoid(g)).reshape(ts, nc, D_CONV)


def _ffn_mix_in_tm(x_tm, g1, wg, wu, wd, g2, w_ssm_t, w_rest, *, layer, ts):
    b, nc, t, _ = x_tm.shape
    tile = lambda w: pl.BlockSpec((1, ts, nc, w), lambda i, j: (i, j, 0, 0))
    return pl.pallas_call(
        _ffn_mix_in_tm_kernel,
        out_shape=(jax.ShapeDtypeStruct((b, t, nc, D_MODEL), F32),
                   jax.ShapeDtypeStruct((N_SSM_GROUPS, t, SSM_GROUP, b * nc), F32),
                   jax.ShapeDtypeStruct((b, t, nc, D_POOL), F32),
                   jax.ShapeDtypeStruct((b, t, nc, D_CONV), F32)),
        grid=(b, t // ts),
        in_specs=[
            pl.BlockSpec((1, nc, ts, D_MODEL), lambda i, j: (i, 0, j, 0)),
            _const_spec((1, D_MODEL)),
            _layer_spec((D_MODEL, D_FF), layer),
            _layer_spec((D_MODEL, D_FF), layer),
            _layer_spec((D_FF, D_MODEL), layer),
            _const_spec((1, D_MODEL)),
            _const_spec((D_SSM, D_MODEL)),
            _const_spec((D_MODEL, D_IN - D_SSM)),
        ],
        out_specs=(tile(D_MODEL),
                   pl.BlockSpec((N_SSM_GROUPS, ts, SSM_GROUP, nc), lambda i, j: (0, j, 0, i)),
                   tile(D_POOL), tile(D_CONV)),
        compiler_params=_cparams("parallel", "parallel"),
    )(x_tm, g1, wg, wu, wd, g2, w_ssm_t, w_rest)


def _ffn_mix_in(x4, g1, wg, wu, wd, g2, w_ssm_t, w_rest, *, layer, bg):
    b, t, nc, _ = x4.shape
    step = lambda w: pl.BlockSpec((bg, None, nc, w), lambda g, i: (g, i, 0, 0))
    return pl.pallas_call(
        _ffn_mix_in_kernel,
        out_shape=(jax.ShapeDtypeStruct((b, t, nc, D_MODEL), F32),
                   jax.ShapeDtypeStruct((N_SSM_GROUPS, t, SSM_GROUP, b * nc), F32),
                   jax.ShapeDtypeStruct((b, t, nc, D_POOL), F32),
                   jax.ShapeDtypeStruct((b, t, nc, D_CONV), F32)),
        grid=(b // bg, t),
        in_specs=[
            step(D_MODEL),
            _const_spec((1, D_MODEL)),
            _layer_spec((D_MODEL, D_FF), layer),
            _layer_spec((D_MODEL, D_FF), layer),
            _layer_spec((D_FF, D_MODEL), layer),
            _const_spec((1, D_MODEL)),
            _const_spec((D_SSM, D_MODEL)),
            _const_spec((D_MODEL, D_IN - D_SSM)),
        ],
        out_specs=(step(D_MODEL),
                   pl.BlockSpec((N_SSM_GROUPS, None, SSM_GROUP, bg * nc), lambda g, i: (0, i, 0, g)),
                   step(D_POOL), step(D_CONV)),
        compiler_params=_cparams("parallel", "parallel"),
    )(x4, g1, wg, wu, wd, g2, w_ssm_t, w_rest)


def _s5_prep_kernel(lr2_ref, li2_ref, ldt_ref, br2_ref, bi2_ref, lrc_ref, lic_ref, crt_ref, cit_ref,
                    zt_ref, wst_ref, gt_ref, cp_ref):
    hi = lax.Precision.HIGHEST
    lane = lax.broadcasted_iota(jnp.int32, (1, LANES), 1)
    first = lane < SSM_STATE

    lr2 = lr2_ref[0]
    li2 = li2_ref[0]
    dt = jnp.exp(ldt_ref[0])
    rho = lr2 * dt
    th = li2 * dt

    def pw_row(e):
        mag = jnp.exp(e * rho)
        return mag * jnp.cos(e * th), mag * jnp.sin(e * th)

    one = jnp.ones((1, 1), F32)
    a_r, a_i = pw_row(one)
    den = lr2 * lr2 + li2 * li2
    z_r = ((a_r - 1.0) * lr2 + a_i * li2) / den
    z_i = (a_i * lr2 - (a_r - 1.0) * li2) / den
    br2 = br2_ref[0]
    bi2 = bi2_ref[0]
    b1 = jnp.where(first, br2, bi2)
    b2 = jnp.where(first, -bi2, br2)
    y1 = z_r * b1 + z_i * b2
    y2 = z_r * b2 - z_i * b1

    e_lo = (S5_SUB - 1 - lax.broadcasted_iota(jnp.int32, (S5_SUB, 1), 0)).astype(F32)
    lo_r, lo_i = pw_row(e_lo)
    lob1 = jnp.concatenate([lo_r[i:i + 1] * y1 + lo_i[i:i + 1] * y2 for i in range(S5_SUB)], axis=0)
    lob2 = jnp.concatenate([lo_r[i:i + 1] * y2 - lo_i[i:i + 1] * y1 for i in range(S5_SUB)], axis=0)
    e_hi = (S5_SUB * (S5_NB - 1 - lax.broadcasted_iota(jnp.int32, (S5_NB, 1), 0))).astype(F32)
    hi_r, hi_i = pw_row(e_hi)
    pb = jnp.concatenate([hi_r[m:m + 1] * lob1 + hi_i[m:m + 1] * lob2 for m in range(S5_NB)], axis=0)
    wst_ref[0] = pb.astype(BF16)

    lrc = lrc_ref[0]
    lic = lic_ref[0]
    rho_c = lrc * dt
    th_c = lic * dt
    mag_c = jnp.exp(rho_c)
    ac_r = mag_c * jnp.cos(th_c)
    ac_i = mag_c * jnp.sin(th_c)
    rc_r, rc_i = [], []
    for half in range(MXU_TILE // LANES):
        e = (lane // SSM_GROUP + half * (LANES // SSM_GROUP)).astype(F32)
        mag = jnp.exp(e * rho_c)
        t_r = mag * jnp.cos(e * th_c)
        t_i = mag * jnp.sin(e * th_c)
        c_r = crt_ref[0, :, half * LANES:(half + 1) * LANES]
        c_i = cit_ref[0, :, half * LANES:(half + 1) * LANES]
        rc_r.append(t_r * c_r - t_i * c_i)
        rc_i.append(t_r * c_i + t_i * c_r)
    ct_stack = jnp.concatenate([crt_ref[0], -cit_ref[0]], axis=0)

    def csq(v_r, v_i):
        return v_r * v_r - v_i * v_i, 2.0 * v_r * v_i

    a16_r, a16_i = ac_r, ac_i
    for _ in range(S5_SUB.bit_length() - 1):
        a16_r, a16_i = csq(a16_r, a16_i)
    g_r, g_i = ac_r, ac_i
    for j in range(S5_NB):
        for half in range(MXU_TILE // LANES):
            lo = j * MXU_TILE + half * LANES
            gt_ref[0, :SSM_STATE, lo:lo + LANES] = (g_r * rc_r[half] - g_i * rc_i[half]).astype(BF16)
            gt_ref[0, SSM_STATE:, lo:lo + LANES] = (-(g_r * rc_i[half] + g_i * rc_r[half])).astype(BF16)
        g_r, g_i = g_r * a16_r - g_i * a16_i, g_r * a16_i + g_i * a16_r

    w_r, w_i = a16_r, a16_i
    for _ in range(S5_NB.bit_length() - 1):
        w_r, w_i = csq(w_r, w_i)
    for r in range(S5_MAX_SCAN_STEPS):
        cp_ref[0, 2 * r * SSM_STATE:(2 * r + 1) * SSM_STATE, :] = w_r
        cp_ref[0, (2 * r + 1) * SSM_STATE:(2 * r + 2) * SSM_STATE, :] = w_i
        w_r, w_i = csq(w_r, w_i)

    off_rows = (S5_CHUNK - S5_SUB) * SSM_GROUP
    if off_rows:
        rc_stack = jnp.concatenate([jnp.concatenate(rc_r, axis=1), -jnp.concatenate(rc_i, axis=1)], axis=0)
        lo_row = (S5_SUB - 1) * SSM_GROUP
        zt_off = jnp.dot(pb[lo_row:lo_row + off_rows], rc_stack, precision=hi, preferred_element_type=F32)
        zt_ref[0, :off_rows, :] = zt_off.astype(BF16)
    kall = jnp.dot(pb[off_rows:], ct_stack, precision=hi, preferred_element_type=F32)
    colblk = lax.broadcasted_iota(jnp.int32, (1, MXU_TILE), 1) // SSM_GROUP
    zd = jnp.zeros((MXU_TILE, MXU_TILE), F32)
    for jj in range(S5_SUB):
        sh = (S5_SUB - 1 - jj) * SSM_GROUP
        shifted = kall if sh == 0 else jnp.concatenate(
            [kall[sh:], jnp.zeros((sh, MXU_TILE), F32)], axis=0)
        zd = jnp.where(colblk == jj, shifted, zd)
    zt_ref[0, off_rows:, :] = zd.astype(BF16)


def _s5_prep(lam_re, lam_im, log_dt, b_re, b_im, c_re, c_im):
    g = N_SSM_GROUPS
    lr2 = jnp.tile(lam_re, (1, 2))[:, None, :]
    li2 = jnp.tile(lam_im, (1, 2))[:, None, :]
    ldt = jnp.broadcast_to(log_dt[:, None, None], (g, 1, LANES))
    br2 = jnp.tile(jnp.swapaxes(b_re, 1, 2), (1, 1, 2))
    bi2 = jnp.tile(jnp.swapaxes(b_im, 1, 2), (1, 1, 2))
    lrc = jnp.broadcast_to(lam_re[:, :, None], (g, SSM_STATE, LANES))
    lic = jnp.broadcast_to(lam_im[:, :, None], (g, SSM_STATE, LANES))
    crt = jnp.tile(jnp.swapaxes(c_re, 1, 2), (1, 1, S5_SUB))
    cit = jnp.tile(jnp.swapaxes(c_im, 1, 2), (1, 1, S5_SUB))

    def spec(*shape):
        return pl.BlockSpec((1,) + shape, lambda i: (i, 0, 0))

    cp_rows = 2 * S5_MAX_SCAN_STEPS * SSM_STATE
    return pl.pallas_call(
        _s5_prep_kernel,
        out_shape=(jax.ShapeDtypeStruct((g, S5_ROWS, MXU_TILE), BF16),
                   jax.ShapeDtypeStruct((g, S5_ROWS, LANES), BF16),
                   jax.ShapeDtypeStruct((g, LANES, S5_ROWS), BF16),
                   jax.ShapeDtypeStruct((g, cp_rows, LANES), F32)),
        grid=(g,),
        in_specs=[spec(1, LANES), spec(1, LANES), spec(1, LANES),
                  spec(SSM_GROUP, LANES), spec(SSM_GROUP, LANES),
                  spec(SSM_STATE, LANES), spec(SSM_STATE, LANES),
                  spec(SSM_STATE, MXU_TILE), spec(SSM_STATE, MXU_TILE)],
        out_specs=(spec(S5_ROWS, MXU_TILE), spec(S5_ROWS, LANES), spec(LANES, S5_ROWS),
                   spec(cp_rows, LANES)),
        compiler_params=_cparams("parallel"),
    )(lr2, li2, ldt, br2, bi2, lrc, lic, crt, cit)


def _s5_main_kernel(u_ref, zw_ref, wst_ref, g_ref, cp_ref, o_ref, *, n_chunks):
    ncols = u_ref.shape[-1]
    u = u_ref[0].reshape(S5_ROWS, ncols).astype(BF16)
    state = jnp.dot(wst_ref[0], u, preferred_element_type=F32)
    s_r, s_i = state[:SSM_STATE], state[SSM_STATE:]
    chunk = lax.broadcasted_iota(jnp.int32, (1, ncols), 1) % n_chunks
    reps = ncols // LANES if ncols > LANES else 1

    def wide(v):
        return jnp.concatenate([v] * reps, axis=1)[:, :ncols]

    r = 0
    while (1 << r) < n_chunks:
        s = 1 << r
        p_r = jnp.where(chunk >= s, pltpu.roll(s_r, s, 1), 0.0)
        p_i = jnp.where(chunk >= s, pltpu.roll(s_i, s, 1), 0.0)
        w_r = wide(cp_ref[0, 2 * r * SSM_STATE:(2 * r + 1) * SSM_STATE, :])
        w_i = wide(cp_ref[0, (2 * r + 1) * SSM_STATE:(2 * r + 2) * SSM_STATE, :])
        s_r, s_i = s_r + w_r * p_r - w_i * p_i, s_i + w_r * p_i + w_i * p_r
        r += 1
    c_r = jnp.where(chunk >= 1, pltpu.roll(s_r, 1, 1), 0.0)
    c_i = jnp.where(chunk >= 1, pltpu.roll(s_i, 1, 1), 0.0)
    carry = jnp.concatenate([c_r, c_i], axis=0).astype(BF16)
    for j in range(S5_NB):
        k = (j + 1) * MXU_TILE
        y = jnp.dot(zw_ref[0, :, (S5_NB - 1 - j) * MXU_TILE:], u[:k], preferred_element_type=F32)
        y = y + jnp.dot(g_ref[0, j * MXU_TILE:(j + 1) * MXU_TILE, :], carry, preferred_element_type=F32)
        o_ref[0, j * S5_SUB:(j + 1) * S5_SUB] = y.reshape(S5_SUB, SSM_GROUP, ncols)


def _s5_main(u4, zw, wst, gm, cp, *, n_chunks):
    g, t, _, ncols = u4.shape
    cp_rows = cp.shape[1]

    def spec(*shape):
        return pl.BlockSpec((1,) + shape, lambda i: (i,) + (0,) * len(shape))

    return pl.pallas_call(
        functools.partial(_s5_main_kernel, n_chunks=n_chunks),
        out_shape=jax.ShapeDtypeStruct((g, t, SSM_GROUP, ncols), F32),
        grid=(g,),
        in_specs=[spec(t, SSM_GROUP, ncols), spec(MXU_TILE, S5_ROWS), spec(LANES, S5_ROWS),
                  spec(S5_ROWS, LANES), spec(cp_rows, LANES)],
        out_specs=spec(t, SSM_GROUP, ncols),
        compiler_params=_cparams("parallel"),
    )(u4, zw, wst, gm, cp)


def _kv_kernel(m_ref, g_ref, wk_ref, wv_ref, k_ref, v_ref):
    m = _rms(m_ref[...], g_ref[...]).astype(BF16)
    k_ref[...] = jnp.dot(m, wk_ref[...], preferred_element_type=F32).astype(BF16)
    v_ref[...] = jnp.dot(m, wv_ref[...], preferred_element_type=F32).astype(BF16)


def _kv(mem2d, norm_g, wk, wv, *, layer, tm):
    n = mem2d.shape[0]
    row = pl.BlockSpec((tm, D_MODEL), lambda i: (i, 0))
    return pl.pallas_call(
        _kv_kernel,
        out_shape=(jax.ShapeDtypeStruct((n, D_MODEL), BF16), jax.ShapeDtypeStruct((n, D_MODEL), BF16)),
        grid=(n // tm,),
        in_specs=[row, _const_spec((1, D_MODEL)), _layer_spec((D_MODEL, D_MODEL), layer),
                  _layer_spec((D_MODEL, D_MODEL), layer)],
        out_specs=(row, row),
        compiler_params=_cparams("parallel"),
    )(mem2d, norm_g, wk, wv)


def _fill_history(ext_ref, src_ref, pad, chunk):
    t = src_ref.shape[1]
    ext_ref[pad:] = src_ref[0]
    for e in range(pad):
        back = e // t + 1
        v = src_ref[0, t - 1 - e % t]
        ext_ref[pad - 1 - e] = jnp.where(chunk >= back, pltpu.roll(v, back, 0), 0.0)


def _pool_conv_tile(uext_ref, hext_ref, p_ref, c_ref, pw_ref, ps_ref, cw_ref, cb_ref, lg_ref, lb_ref,
                    step0, ts, t):
    nc = uext_ref.shape[1]
    chunk = lax.broadcasted_iota(jnp.int32, (nc, 1), 0)
    low = lax.broadcasted_iota(jnp.int32, (1, LANES), 1) < POOL_GROUP
    first_tap = CONV_PAD - (CONV_WIDTH - 1)
    groups_per_tile = LANES // POOL_GROUP

    def one_step(s, carry):
        i = step0 + s
        rows = pl.ds(pl.multiple_of(s * nc, nc), nc)
        pos = (chunk * t + i + 1).astype(F32)
        for lt in range(D_POOL // LANES):
            lanes = slice(lt * LANES, (lt + 1) * LANES)
            w_lo, w_hi = POOL_WINDOWS[groups_per_tile * lt:groups_per_tile * (lt + 1)]
            cur = uext_ref[POOL_PAD + i, :, lanes]
            run = cur
            saved = {}
            for d in range(1, w_hi):
                run = run + uext_ref[POOL_PAD + i - d, :, lanes]
                if d + 1 in (w_lo, w_hi):
                    saved[d + 1] = run
            total = jnp.where(low, saved[w_lo], saved[w_hi])
            count = jnp.minimum(pos, jnp.where(low, float(w_lo), float(w_hi)))
            p_ref[rows, lanes] = total / count - cur
        return carry

    lax.fori_loop(0, ts, one_step, 0)

    rb = min(CONV_ROWS, nc)
    for lt in range(D_CONV // LANES):
        lanes = slice(lt * LANES, (lt + 1) * LANES)

        def row_block(r, carry, lanes=lanes):
            r0 = pl.multiple_of(r * rb, rb)
            accs = [jnp.zeros((rb, LANES), F32) + cb_ref[:, lanes] for _ in range(ts)]
            for m in range(ts + CONV_WIDTH - 1):
                h = hext_ref[step0 + first_tap + m, pl.ds(r0, rb), lanes]
                for s in range(ts):
                    k = m - s
                    if 0 <= k < CONV_WIDTH:
                        accs[s] = accs[s] + cw_ref[k:k + 1, lanes] * h
            for s in range(ts):
                c_ref[pl.ds(pl.multiple_of(s * nc + r * rb, rb), rb), lanes] = accs[s]
            return carry

        lax.fori_loop(0, nc // rb, row_block, 0)
    y_pool = jnp.dot(p_ref[...].astype(BF16), pw_ref[...], preferred_element_type=F32) * ps_ref[...]
    conv = c_ref[...]
    mu = jnp.mean(conv, axis=-1, keepdims=True)
    cen = conv - mu
    var = jnp.mean(cen * cen, axis=-1, keepdims=True)
    hn = cen * lax.rsqrt(var + EPS) * lg_ref[...] + lb_ref[...]
    return jnp.concatenate([y_pool, hn * jax.nn.sigmoid(hn)], axis=1).astype(BF16)


def _mix_attn_kernel(x_ref, y4_ref, u4_ref, up_ref, hc_ref, k_ref, v_ref,
                     pw_ref, ps_ref, cw_ref, cb_ref, lg_ref, lb_ref,
                     d_ref, wglut_ref, wo1_ref, wo2_ref, ga_ref, wq_ref, wo_ref,
                     o_ref, uext_ref, hext_ref, p_ref, c_ref):
    _, ts, nc, _ = x_ref.shape
    t = up_ref.shape[1]
    j = pl.program_id(1)
    chunk = lax.broadcasted_iota(jnp.int32, (nc, 1), 0)

    @pl.when(j == 0)
    def _():
        _fill_history(uext_ref, up_ref, POOL_PAD, chunk)
        _fill_history(hext_ref, hc_ref, CONV_PAD, chunk)

    y_pc = _pool_conv_tile(uext_ref, hext_ref, p_ref, c_ref, pw_ref, ps_ref, cw_ref, cb_ref, lg_ref, lb_ref,
                           j * ts, ts, t)

    yt = jnp.concatenate([y4_ref[:, s].reshape(D_SSM, nc) for s in range(ts)], axis=1)
    ut = jnp.concatenate([u4_ref[:, s].reshape(D_SSM, nc) for s in range(ts)], axis=1)
    yt = jax.nn.gelu(yt + d_ref[...] * ut, approximate=True)
    gate = jax.nn.sigmoid(jnp.dot(wglut_ref[...], yt.astype(BF16), preferred_element_type=F32))
    y_ssm = (yt * gate).T.astype(BF16)

    x = x_ref[0].reshape(ts * nc, D_MODEL)
    x = x + jnp.dot(y_ssm, wo1_ref[...], preferred_element_type=F32) \
        + jnp.dot(y_pc, wo2_ref[...], preferred_element_type=F32)

    h = _rms(x, ga_ref[...]).astype(BF16)
    q = jnp.dot(h, wq_ref[...], preferred_element_type=F32) * (XHEAD_DIM ** -0.5)
    heads = []
    for hd in range(N_XHEADS):
        sl = slice(hd * XHEAD_DIM, (hd + 1) * XHEAD_DIM)
        sc = lax.dot_general(q[:, sl].astype(BF16), k_ref[0, :, sl], (((1,), (1,)), ((), ())),
                             preferred_element_type=F32)
        pr = jnp.exp(sc - jnp.max(sc, axis=-1, keepdims=True))
        l = jnp.sum(pr, axis=-1, keepdims=True)
        heads.append(jnp.dot(pr.astype(BF16), v_ref[0, :, sl], preferred_element_type=F32) / l)
    o = jnp.concatenate(heads, axis=-1).astype(BF16)
    x = x + jnp.dot(o, wo_ref[...], preferred_element_type=F32)
    o_ref[0] = x.reshape(ts, nc, D_MODEL)


def _mix_attn(x4, y4, u4, up4, hc4, k3d, v3d, pool_w_bd, pool_scale, conv_w, conv_b, ln_g, ln_b,
              d_col, w_glu_t, w_out1, w_out2, ga, wq, wo, *, layer, ts):
    b, t, nc, _ = x4.shape
    xspec = pl.BlockSpec((1, ts, nc, D_MODEL), lambda i, j: (i, j, 0, 0))
    chan = pl.BlockSpec((N_SSM_GROUPS, ts, SSM_GROUP, nc), lambda i, j: (0, j, 0, i))
    whole = lambda w: pl.BlockSpec((1, t, nc, w), lambda i, j: (i, 0, 0, 0))
    mem = pl.BlockSpec((1, MEM_LEN, D_MODEL), lambda i, j: (i, 0, 0))
    return pl.pallas_call(
        _mix_attn_kernel,
        out_shape=jax.ShapeDtypeStruct((b, t, nc, D_MODEL), F32),
        grid=(b, t // ts),
        in_specs=[
            xspec, chan, chan, whole(D_POOL), whole(D_CONV), mem, mem,
            _const_spec((D_POOL, D_POOL)), _const_spec((1, D_POOL)),
            _const_spec((CONV_WIDTH, D_CONV)), _const_spec((1, D_CONV)),
            _const_spec((1, D_CONV)), _const_spec((1, D_CONV)),
            _const_spec((D_SSM, 1)), _const_spec((D_SSM, D_SSM)),
            _const_spec((D_SSM, D_MODEL)), _const_spec((D_PC, D_MODEL)),
            _const_spec((1, D_MODEL)), _layer_spec((D_MODEL, D_MODEL), layer),
            _layer_spec((D_MODEL, D_MODEL), layer),
        ],
        out_specs=xspec,
        scratch_shapes=[pltpu.VMEM((POOL_PAD + t, nc, D_POOL), F32),
                        pltpu.VMEM((CONV_PAD + t, nc, D_CONV), F32),
                        pltpu.VMEM((ts * nc, D_POOL), F32),
                        pltpu.VMEM((ts * nc, D_CONV), F32)],
        compiler_params=_cparams("parallel", "arbitrary"),
    )(x4, y4, u4, up4, hc4, k3d, v3d, pool_w_bd, pool_scale, conv_w, conv_b, ln_g, ln_b,
      d_col, w_glu_t, w_out1, w_out2, ga, wq, wo)


def _ffn_kernel(x_ref, g_ref, wg_ref, wu_ref, wd_ref, o_ref):
    o_ref[...] = _ffn_body(x_ref[...], g_ref, wg_ref, wu_ref, wd_ref)


def _ffn(x2d, norm_g, wg, wu, wd, *, layer, tm):
    n = x2d.shape[0]
    row = pl.BlockSpec((tm, D_MODEL), lambda i: (i, 0))
    return pl.pallas_call(
        _ffn_kernel,
        out_shape=jax.ShapeDtypeStruct((n, D_MODEL), F32),
        grid=(n // tm,),
        in_specs=[row, _const_spec((1, D_MODEL)), _layer_spec((D_MODEL, D_FF), layer),
                  _layer_spec((D_MODEL, D_FF), layer), _layer_spec((D_FF, D_MODEL), layer)],
        out_specs=row,
        compiler_params=_cparams("parallel"),
    )(x2d, norm_g, wg, wu, wd)


def _ffn_out_kernel(x_ref, g_ref, wg_ref, wu_ref, wd_ref, fg_ref, o_ref):
    _, ts, nc, _ = x_ref.shape
    y = _ffn_body(x_ref[0].reshape(ts * nc, D_MODEL), g_ref, wg_ref, wu_ref, wd_ref)
    y = _rms(y, fg_ref[...]).reshape(ts, nc, D_MODEL)
    o_ref[0] = jnp.swapaxes(y, 0, 1)


def _ffn_out(x4, norm_g, wg, wu, wd, final_g, *, layer, ts):
    b, t, nc, _ = x4.shape
    return pl.pallas_call(
        _ffn_out_kernel,
        out_shape=jax.ShapeDtypeStruct((b, nc, t, D_MODEL), F32),
        grid=(b, t // ts),
        in_specs=[pl.BlockSpec((1, ts, nc, D_MODEL), lambda i, j: (i, j, 0, 0)),
                  _const_spec((1, D_MODEL)), _layer_spec((D_MODEL, D_FF), layer),
                  _layer_spec((D_MODEL, D_FF), layer), _layer_spec((D_FF, D_MODEL), layer),
                  _const_spec((1, D_MODEL))],
        out_specs=pl.BlockSpec((1, nc, ts, D_MODEL), lambda i, j: (i, 0, j, 0)),
        compiler_params=_cparams("parallel", "parallel"),
    )(x4, norm_g, wg, wu, wd, final_g)


def _tile(n, pref):
    t = max(1, min(n, pref))
    assert n % t == 0, (n, t)
    return t


def kernel(x, mem, ffn1_norm, ffn1_w_gate, ffn1_w_up, ffn1_w_down, mix_norm, w_in, w_out, ssm_lambda_re, ssm_lambda_im, ssm_log_dt, ssm_b_re, ssm_b_im, ssm_c_re, ssm_c_im, ssm_d, ssm_w_glu, pool_w, pool_scale, conv_w, conv_b, conv_ln_g, conv_ln_b, xattn_norm, mem_norm, xattn_wq, xattn_wk, xattn_wv, xattn_wo, ffn2_norm, ffn2_w_gate, ffn2_w_up, ffn2_w_down, final_norm):
    bsz, seq, _ = x.shape
    depth = w_in.shape[0]
    n = bsz * seq
    assert seq % S5_CHUNK == 0
    n_chunks = seq // S5_CHUNK
    assert n_chunks & (n_chunks - 1) == 0 and n_chunks <= (1 << S5_MAX_SCAN_STEPS)
    bg = _tile(bsz, FFN_TOKEN_TILE // n_chunks)
    ts = _tile(S5_CHUNK, TOKEN_TILE // n_chunks)
    to = _tile(S5_CHUNK, SUBLANES)
    tm = _tile(n, FFN_TOKEN_TILE)
    tkv = _tile(bsz * MEM_LEN, TOKEN_TILE)
    bf = lambda w: w.astype(BF16)
    row = lambda v: v.reshape(1, -1)

    mem2d = mem.reshape(bsz * MEM_LEN, D_MODEL)
    f1g, f1u, f1d = bf(ffn1_w_gate), bf(ffn1_w_up), bf(ffn1_w_down)
    f2g, f2u, f2d = bf(ffn2_w_gate), bf(ffn2_w_up), bf(ffn2_w_down)
    wq, wk, wv, wo = bf(xattn_wq), bf(xattn_wk), bf(xattn_wv), bf(xattn_wo)
    for l in range(depth):
        in_args = (row(ffn1_norm[l]), f1g, f1u, f1d, row(mix_norm[l]), bf(w_in[l, :, :D_SSM].T),
                   bf(w_in[l, :, D_SSM:]))
        if l == 0:
            x4, u4, up4, hc4 = _ffn_mix_in_tm(x.reshape(bsz, n_chunks, S5_CHUNK, D_MODEL), *in_args,
                                              layer=l, ts=to)
        else:
            x4, u4, up4, hc4 = _ffn_mix_in(x4, *in_args, layer=l, bg=bg)

        zt, wst_t, gt, cp = _s5_prep(ssm_lambda_re[l], ssm_lambda_im[l], ssm_log_dt[l], ssm_b_re[l],
                                     ssm_b_im[l], ssm_c_re[l], ssm_c_im[l])
        y4 = _s5_main(u4, jnp.swapaxes(zt, 1, 2), jnp.swapaxes(wst_t, 1, 2), jnp.swapaxes(gt, 1, 2), cp,
                      n_chunks=n_chunks)

        k2d, v2d = _kv(mem2d, row(mem_norm[l]), wk, wv, layer=l, tm=tkv)
        pool_bd = jax.scipy.linalg.block_diag(*[pool_w[l, g] for g in range(len(POOL_WINDOWS))])
        x4 = _mix_attn(x4, y4, u4, up4, hc4, k2d.reshape(bsz, MEM_LEN, D_MODEL),
                       v2d.reshape(bsz, MEM_LEN, D_MODEL), bf(pool_bd), row(pool_scale[l]), conv_w[l],
                       row(conv_b[l]), row(conv_ln_g[l]), row(conv_ln_b[l]), ssm_d[l].reshape(D_SSM, 1),
                       bf(ssm_w_glu[l].T), bf(w_out[l, :D_SSM]), bf(w_out[l, D_SSM:]),
                       row(xattn_norm[l]), wq, wo, layer=l, ts=ts)

        if l == depth - 1:
            out = _ffn_out(x4, row(ffn2_norm[l]), f2g, f2u, f2d, row(final_norm), layer=l, ts=to)
            return out.reshape(bsz, seq, D_MODEL)
        x2d = _ffn(x4.reshape(n, D_MODEL), row(ffn2_norm[l]), f2g, f2u, f2d, layer=l, tm=tm)
        x4 = x2d.reshape(bsz, S5_CHUNK, n_chunks, D_MODEL)
```

```python
import functools

import jax
import jax.numpy as jnp
from jax import lax
from jax.experimental import pallas as pl
from jax.experimental.pallas import tpu as pltpu

F32 = jnp.float32
BF16 = jnp.bfloat16

D_MODEL = 1024
MEM_LEN = 256
D_SSM = 384
D_POOL = 256
D_CONV = 384
D_PC = D_POOL + D_CONV
SSM_GROUP = 16
N_SSM_GROUPS = 24
SSM_STATE = 64
POOL_WINDOWS = (2, 4, 8, 16)
POOL_GROUP = 64
CONV_WIDTH = 31
D_IN = D_SSM + D_POOL + 2 * D_CONV
D_FF = 2816
N_XHEADS = 4
XHEAD_DIM = 256
EPS = 1e-6

SUBLANES = 8
LANES = 128
MXU_TILE = 256

TOKEN_TILE = 512
FFN_TOKEN_TILE = 1024
FF_CHUNK = MXU_TILE
S5_SUB = MXU_TILE // SSM_GROUP
S5_CHUNK = S5_SUB
S5_NB = S5_CHUNK // S5_SUB
S5_ROWS = S5_CHUNK * SSM_GROUP
S5_MAX_SCAN_STEPS = 7
CONV_PAD = 32
POOL_PAD = 16
VMEM_LIMIT = 56 * 1024 * 1024


def _cparams(*sem):
    return pltpu.CompilerParams(dimension_semantics=sem, vmem_limit_bytes=VMEM_LIMIT)


def _const_spec(shape):
    nd = len(shape)
    return pl.BlockSpec(shape, lambda *_: (0,) * nd, pipeline_mode=pl.Buffered(1))


def _layer_spec(shape, layer):
    nd = len(shape)
    return pl.BlockSpec((None,) + shape, lambda *_: (layer,) + (0,) * nd, pipeline_mode=pl.Buffered(1))


def _rms(x, g):
    ms = jnp.mean(x * x, axis=-1, keepdims=True)
    return x * lax.rsqrt(ms + EPS) * g


def _ffn_body(x, g_ref, wg_ref, wu_ref, wd_ref):
    h = _rms(x, g_ref[...]).astype(BF16)
    acc = jnp.zeros(x.shape, F32)
    for c in range(D_FF // FF_CHUNK):
        sl = slice(c * FF_CHUNK, (c + 1) * FF_CHUNK)
        gate = jnp.dot(h, wg_ref[:, sl], preferred_element_type=F32)
        up = jnp.dot(h, wu_ref[:, sl], preferred_element_type=F32)
        act = (gate * jax.nn.sigmoid(gate) * up).astype(BF16)
        acc = acc + jnp.dot(act, wd_ref[sl, :], preferred_element_type=F32)
    return x + 0.5 * acc


def _ffn_mix_in_kernel(x_ref, g1_ref, wg_ref, wu_ref, wd_ref, g2_ref, wst_ref, wrest_ref,
                       xo_ref, u4_ref, up_ref, hc_ref):
    b, nc, _ = x_ref.shape
    y = _ffn_body(x_ref[...].reshape(b * nc, D_MODEL), g1_ref, wg_ref, wu_ref, wd_ref)
    xo_ref[...] = y.reshape(b, nc, D_MODEL)
    h = _rms(y, g2_ref[...]).astype(BF16)
    ut = lax.dot_general(wst_ref[...], h, (((1,), (1,)), ((), ())), preferred_element_type=F32)
    u4_ref[...] = ut.reshape(N_SSM_GROUPS, SSM_GROUP, b * nc)
    z = jnp.dot(h, wrest_ref[...], preferred_element_type=F32)
    up_ref[...] = z[:, :D_POOL].reshape(b, nc, D_POOL)
    v = z[:, D_POOL:D_POOL + D_CONV]
    g = z[:, D_POOL + D_CONV:]
    hc_ref[...] = (v * jax.nn.sigmoid(g)).reshape(b, nc, D_CONV)


def _ffn_mix_in_tm_kernel(x_ref, g1_ref, wg_ref, wu_ref, wd_ref, g2_ref, wst_ref, wrest_ref,
                          xo_ref, u4_ref, up_ref, hc_ref):
    _, nc, ts, _ = x_ref.shape
    x = jnp.swapaxes(x_ref[0], 0, 1).reshape(ts * nc, D_MODEL)
    y = _ffn_body(x, g1_ref, wg_ref, wu_ref, wd_ref)
    xo_ref[0] = y.reshape(ts, nc, D_MODEL)
    h = _rms(y, g2_ref[...]).astype(BF16)
    ut = lax.dot_general(wst_ref[...], h, (((1,), (1,)), ((), ())), preferred_element_type=F32)
    for s in range(ts):
        u4_ref[:, s] = ut[:, s * nc:(s + 1) * nc].reshape(N_SSM_GROUPS, SSM_GROUP, nc)
    z = jnp.dot(h, wrest_ref[...], preferred_element_type=F32)
    up_ref[0] = z[:, :D_POOL].reshape(ts, nc, D_POOL)
    v = z[:, D_POOL:D_POOL + D_CONV]
    g = z[:, D_POOL + D_CONV:]
    hc_ref[0] = (v * jax.nn.sigmoid(g)).reshape(ts, nc, D_CONV)


def _ffn_mix_in_tm(x_tm, g1, wg, wu, wd, g2, w_ssm_t, w_rest, *, layer, ts):
    b, nc, t, _ = x_tm.shape
    tile = lambda w: pl.BlockSpec((1, ts, nc, w), lambda i, j: (i, j, 0, 0))
    return pl.pallas_call(
        _ffn_mix_in_tm_kernel,
        out_shape=(jax.ShapeDtypeStruct((b, t, nc, D_MODEL), F32),
                   jax.ShapeDtypeStruct((N_SSM_GROUPS, t, SSM_GROUP, b * nc), F32),
                   jax.ShapeDtypeStruct((b, t, nc, D_POOL), F32),
                   jax.ShapeDtypeStruct((b, t, nc, D_CONV), F32)),
        grid=(b, t // ts),
        in_specs=[
            pl.BlockSpec((1, nc, ts, D_MODEL), lambda i, j: (i, 0, j, 0)),
            _const_spec((1, D_MODEL)),
            _layer_spec((D_MODEL, D_FF), layer),
            _layer_spec((D_MODEL, D_FF), layer),
            _layer_spec((D_FF, D_MODEL), layer),
            _const_spec((1, D_MODEL)),
            _const_spec((D_SSM, D_MODEL)),
            _const_spec((D_MODEL, D_IN - D_SSM)),
        ],
        out_specs=(tile(D_MODEL),
                   pl.BlockSpec((N_SSM_GROUPS, ts, SSM_GROUP, nc), lambda i, j: (0, j, 0, i)),
                   tile(D_POOL), tile(D_CONV)),
        compiler_params=_cparams("parallel", "parallel"),
    )(x_tm, g1, wg, wu, wd, g2, w_ssm_t, w_rest)


def _ffn_mix_in(x4, g1, wg, wu, wd, g2, w_ssm_t, w_rest, *, layer, bg):
    b, t, nc, _ = x4.shape
    step = lambda w: pl.BlockSpec((bg, None, nc, w), lambda g, i: (g, i, 0, 0))
    return pl.pallas_call(
        _ffn_mix_in_kernel,
        out_shape=(jax.ShapeDtypeStruct((b, t, nc, D_MODEL), F32),
                   jax.ShapeDtypeStruct((N_SSM_GROUPS, t, SSM_GROUP, b * nc), F32),
                   jax.ShapeDtypeStruct((b, t, nc, D_POOL), F32),
                   jax.ShapeDtypeStruct((b, t, nc, D_CONV), F32)),
        grid=(b // bg, t),
        in_specs=[
            step(D_MODEL),
            _const_spec((1, D_MODEL)),
            _layer_spec((D_MODEL, D_FF), layer),
            _layer_spec((D_MODEL, D_FF), layer),
            _layer_spec((D_FF, D_MODEL), layer),
            _const_spec((1, D_MODEL)),
            _const_spec((D_SSM, D_MODEL)),
            _const_spec((D_MODEL, D_IN - D_SSM)),
        ],
        out_specs=(step(D_MODEL),
                   pl.BlockSpec((N_SSM_GROUPS, None, SSM_GROUP, bg * nc), lambda g, i: (0, i, 0, g)),
                   step(D_POOL), step(D_CONV)),
        compiler_params=_cparams("parallel", "parallel"),
    )(x4, g1, wg, wu, wd, g2, w_ssm_t, w_rest)


def _s5_prep_kernel(lr2_ref, li2_ref, ldt_ref, br2_ref, bi2_ref, lrc_ref, lic_ref, crt_ref, cit_ref,
                    zt_ref, wst_ref, gt_ref, cp_ref):
    hi = lax.Precision.HIGHEST
    lane = lax.broadcasted_iota(jnp.int32, (1, LANES), 1)
    first = lane < SSM_STATE

    lr2 = lr2_ref[0]
    li2 = li2_ref[0]
    dt = jnp.exp(ldt_ref[0])
    rho = lr2 * dt
    th = li2 * dt

    def pw_row(e):
        mag = jnp.exp(e * rho)
        return mag * jnp.cos(e * th), mag * jnp.sin(e * th)

    one = jnp.ones((1, 1), F32)
    a_r, a_i = pw_row(one)
    den = lr2 * lr2 + li2 * li2
    z_r = ((a_r - 1.0) * lr2 + a_i * li2) / den
    z_i = (a_i * lr2 - (a_r - 1.0) * li2) / den
    br2 = br2_ref[0]
    bi2 = bi2_ref[0]
    b1 = jnp.where(first, br2, bi2)
    b2 = jnp.where(first, -bi2, br2)
    y1 = z_r * b1 + z_i * b2
    y2 = z_r * b2 - z_i * b1

    e_lo = (S5_SUB - 1 - lax.broadcasted_iota(jnp.int32, (S5_SUB, 1), 0)).astype(F32)
    lo_r, lo_i = pw_row(e_lo)
    lob1 = jnp.concatenate([lo_r[i:i + 1] * y1 + lo_i[i:i + 1] * y2 for i in range(S5_SUB)], axis=0)
    lob2 = jnp.concatenate([lo_r[i:i + 1] * y2 - lo_i[i:i + 1] * y1 for i in range(S5_SUB)], axis=0)
    e_hi = (S5_SUB * (S5_NB - 1 - lax.broadcasted_iota(jnp.int32, (S5_NB, 1), 0))).astype(F32)
    hi_r, hi_i = pw_row(e_hi)
    pb = jnp.concatenate([hi_r[m:m + 1] * lob1 + hi_i[m:m + 1] * lob2 for m in range(S5_NB)], axis=0)
    wst_ref[0] = pb.astype(BF16)

    lrc = lrc_ref[0]
    lic = lic_ref[0]
    rho_c = lrc * dt
    th_c = lic * dt
    mag_c = jnp.exp(rho_c)
    ac_r = mag_c * jnp.cos(th_c)
    ac_i = mag_c * jnp.sin(th_c)
    rc_r, rc_i = [], []
    for half in range(MXU_TILE // LANES):
        e = (lane // SSM_GROUP + half * (LANES // SSM_GROUP)).astype(F32)
        mag = jnp.exp(e * rho_c)
        t_r = mag * jnp.cos(e * th_c)
        t_i = mag * jnp.sin(e * th_c)
        c_r = crt_ref[0, :, half * LANES:(half + 1) * LANES]
        c_i = cit_ref[0, :, half * LANES:(half + 1) * LANES]
        rc_r.append(t_r * c_r - t_i * c_i)
        rc_i.append(t_r * c_i + t_i * c_r)
    ct_stack = jnp.concatenate([crt_ref[0], -cit_ref[0]], axis=0)

    def csq(v_r, v_i):
        return v_r * v_r - v_i * v_i, 2.0 * v_r * v_i

    a16_r, a16_i = ac_r, ac_i
    for _ in range(S5_SUB.bit_length() - 1):
        a16_r, a16_i = csq(a16_r, a16_i)
    g_r, g_i = ac_r, ac_i
    for j in range(S5_NB):
        for half in range(MXU_TILE // LANES):
            lo = j * MXU_TILE + half * LANES
            gt_ref[0, :SSM_STATE, lo:lo + LANES] = (g_r * rc_r[half] - g_i * rc_i[half]).astype(BF16)
            gt_ref[0, SSM_STATE:, lo:lo + LANES] = (-(g_r * rc_i[half] + g_i * rc_r[half])).astype(BF16)
        g_r, g_i = g_r * a16_r - g_i * a16_i, g_r * a16_i + g_i * a16_r

    w_r, w_i = a16_r, a16_i
    for _ in range(S5_NB.bit_length() - 1):
        w_r, w_i = csq(w_r, w_i)
    for r in range(S5_MAX_SCAN_STEPS):
        cp_ref[0, 2 * r * SSM_STATE:(2 * r + 1) * SSM_STATE, :] = w_r
        cp_ref[0, (2 * r + 1) * SSM_STATE:(2 * r + 2) * SSM_STATE, :] = w_i
        w_r, w_i = csq(w_r, w_i)

    off_rows = (S5_CHUNK - S5_SUB) * SSM_GROUP
    if off_rows:
        rc_stack = jnp.concatenate([jnp.concatenate(rc_r, axis=1), -jnp.concatenate(rc_i, axis=1)], axis=0)
        lo_row = (S5_SUB - 1) * SSM_GROUP
        zt_off = jnp.dot(pb[lo_row:lo_row + off_rows], rc_stack, precision=hi, preferred_element_type=F32)
        zt_ref[0, :off_rows, :] = zt_off.astype(BF16)
    kall = jnp.dot(pb[off_rows:], ct_stack, precision=hi, preferred_element_type=F32)
    colblk = lax.broadcasted_iota(jnp.int32, (1, MXU_TILE), 1) // SSM_GROUP
    zd = jnp.zeros((MXU_TILE, MXU_TILE), F32)
    for jj in range(S5_SUB):
        sh = (S5_SUB - 1 - jj) * SSM_GROUP
        shifted = kall if sh == 0 else jnp.concatenate(
            [kall[sh:], jnp.zeros((sh, MXU_TILE), F32)], axis=0)
        zd = jnp.where(colblk == jj, shifted, zd)
    zt_ref[0, off_rows:, :] = zd.astype(BF16)


def _s5_prep(lam_re, lam_im, log_dt, b_re, b_im, c_re, c_im):
    g = N_SSM_GROUPS
    lr2 = jnp.tile(lam_re, (1, 2))[:, None, :]
    li2 = jnp.tile(lam_im, (1, 2))[:, None, :]
    ldt = jnp.broadcast_to(log_dt[:, None, None], (g, 1, LANES))
    br2 = jnp.tile(jnp.swapaxes(b_re, 1, 2), (1, 1, 2))
    bi2 = jnp.tile(jnp.swapaxes(b_im, 1, 2), (1, 1, 2))
    lrc = jnp.broadcast_to(lam_re[:, :, None], (g, SSM_STATE, LANES))
    lic = jnp.broadcast_to(lam_im[:, :, None], (g, SSM_STATE, LANES))
    crt = jnp.tile(jnp.swapaxes(c_re, 1, 2), (1, 1, S5_SUB))
    cit = jnp.tile(jnp.swapaxes(c_im, 1, 2), (1, 1, S5_SUB))

    def spec(*shape):
        return pl.BlockSpec((1,) + shape, lambda i: (i, 0, 0))

    cp_rows = 2 * S5_MAX_SCAN_STEPS * SSM_STATE
    return pl.pallas_call(
        _s5_prep_kernel,
        out_shape=(jax.ShapeDtypeStruct((g, S5_ROWS, MXU_TILE), BF16),
                   jax.ShapeDtypeStruct((g, S5_ROWS, LANES), BF16),
                   jax.ShapeDtypeStruct((g, LANES, S5_ROWS), BF16),
                   jax.ShapeDtypeStruct((g, cp_rows, LANES), F32)),
        grid=(g,),
        in_specs=[spec(1, LANES), spec(1, LANES), spec(1, LANES),
                  spec(SSM_GROUP, LANES), spec(SSM_GROUP, LANES),
                  spec(SSM_STATE, LANES), spec(SSM_STATE, LANES),
                  spec(SSM_STATE, MXU_TILE), spec(SSM_STATE, MXU_TILE)],
        out_specs=(spec(S5_ROWS, MXU_TILE), spec(S5_ROWS, LANES), spec(LANES, S5_ROWS),
                   spec(cp_rows, LANES)),
        compiler_params=_cparams("parallel"),
    )(lr2, li2, ldt, br2, bi2, lrc, lic, crt, cit)


def _s5_main_kernel(u_ref, zw_ref, wst_ref, g_ref, cp_ref, o_ref, *, n_chunks):
    ncols = u_ref.shape[-1]
    u = u_ref[0].reshape(S5_ROWS, ncols).astype(BF16)
    state = jnp.dot(wst_ref[0], u, preferred_element_type=F32)
    s_r, s_i = state[:SSM_STATE], state[SSM_STATE:]
    chunk = lax.broadcasted_iota(jnp.int32, (1, ncols), 1) % n_chunks
    reps = ncols // LANES if ncols > LANES else 1

    def wide(v):
        return jnp.concatenate([v] * reps, axis=1)[:, :ncols]

    r = 0
    while (1 << r) < n_chunks:
        s = 1 << r
        p_r = jnp.where(chunk >= s, pltpu.roll(s_r, s, 1), 0.0)
        p_i = jnp.where(chunk >= s, pltpu.roll(s_i, s, 1), 0.0)
        w_r = wide(cp_ref[0, 2 * r * SSM_STATE:(2 * r + 1) * SSM_STATE, :])
        w_i = wide(cp_ref[0, (2 * r + 1) * SSM_STATE:(2 * r + 2) * SSM_STATE, :])
        s_r, s_i = s_r + w_r * p_r - w_i * p_i, s_i + w_r * p_i + w_i * p_r
        r += 1
    c_r = jnp.where(chunk >= 1, pltpu.roll(s_r, 1, 1), 0.0)
    c_i = jnp.where(chunk >= 1, pltpu.roll(s_i, 1, 1), 0.0)
    carry = jnp.concatenate([c_r, c_i], axis=0).astype(BF16)
    for j in range(S5_NB):
        k = (j + 1) * MXU_TILE
        y = jnp.dot(zw_ref[0, :, (S5_NB - 1 - j) * MXU_TILE:], u[:k], preferred_element_type=F32)
        y = y + jnp.dot(g_ref[0, j * MXU_TILE:(j + 1) * MXU_TILE, :], carry, preferred_element_type=F32)
        o_ref[0, j * S5_SUB:(j + 1) * S5_SUB] = y.reshape(S5_SUB, SSM_GROUP, ncols)


def _s5_main(u4, zw, wst, gm, cp, *, n_chunks):
    g, t, _, ncols = u4.shape
    cp_rows = cp.shape[1]

    def spec(*shape):
        return pl.BlockSpec((1,) + shape, lambda i: (i,) + (0,) * len(shape))

    return pl.pallas_call(
        functools.partial(_s5_main_kernel, n_chunks=n_chunks),
        out_shape=jax.ShapeDtypeStruct((g, t, SSM_GROUP, ncols), F32),
        grid=(g,),
        in_specs=[spec(t, SSM_GROUP, ncols), spec(MXU_TILE, S5_ROWS), spec(LANES, S5_ROWS),
                  spec(S5_ROWS, LANES), spec(cp_rows, LANES)],
        out_specs=spec(t, SSM_GROUP, ncols),
        compiler_params=_cparams("parallel"),
    )(u4, zw, wst, gm, cp)


def _kv_kernel(m_ref, g_ref, wk_ref, wv_ref, k_ref, v_ref):
    m = _rms(m_ref[...], g_ref[...]).astype(BF16)
    k_ref[...] = jnp.dot(m, wk_ref[...], preferred_element_type=F32).astype(BF16)
    v_ref[...] = jnp.dot(m, wv_ref[...], preferred_element_type=F32).astype(BF16)


def _kv(mem2d, norm_g, wk, wv, *, layer, tm):
    n = mem2d.shape[0]
    row = pl.BlockSpec((tm, D_MODEL), lambda i: (i, 0))
    return pl.pallas_call(
        _kv_kernel,
        out_shape=(jax.ShapeDtypeStruct((n, D_MODEL), BF16), jax.ShapeDtypeStruct((n, D_MODEL), BF16)),
        grid=(n // tm,),
        in_specs=[row, _const_spec((1, D_MODEL)), _layer_spec((D_MODEL, D_MODEL), layer),
                  _layer_spec((D_MODEL, D_MODEL), layer)],
        out_specs=(row, row),
        compiler_params=_cparams("parallel"),
    )(mem2d, norm_g, wk, wv)


def _fill_history(ext_ref, src_ref, pad, chunk):
    t = src_ref.shape[1]
    ext_ref[pad:] = src_ref[0]
    for e in range(pad):
        back = e // t + 1
        v = src_ref[0, t - 1 - e % t]
        ext_ref[pad - 1 - e] = jnp.where(chunk >= back, pltpu.roll(v, back, 0), 0.0)


def _pool_conv_tile(uext_ref, hext_ref, p_ref, c_ref, pw_ref, ps_ref, cw_ref, cb_ref, lg_ref, lb_ref,
                    step0, ts, t):
    nc = uext_ref.shape[1]
    chunk = lax.broadcasted_iota(jnp.int32, (nc, 1), 0)
    low = lax.broadcasted_iota(jnp.int32, (1, LANES), 1) < POOL_GROUP
    first_tap = CONV_PAD - (CONV_WIDTH - 1)
    groups_per_tile = LANES // POOL_GROUP

    def one_step(s, carry):
        i = step0 + s
        rows = pl.ds(pl.multiple_of(s * nc, nc), nc)
        pos = (chunk * t + i + 1).astype(F32)
        for lt in range(D_POOL // LANES):
            lanes = slice(lt * LANES, (lt + 1) * LANES)
            w_lo, w_hi = POOL_WINDOWS[groups_per_tile * lt:groups_per_tile * (lt + 1)]
            cur = uext_ref[POOL_PAD + i, :, lanes]
            run = cur
            saved = {}
            for d in range(1, w_hi):
                run = run + uext_ref[POOL_PAD + i - d, :, lanes]
                if d + 1 in (w_lo, w_hi):
                    saved[d + 1] = run
            total = jnp.where(low, saved[w_lo], saved[w_hi])
            count = jnp.minimum(pos, jnp.where(low, float(w_lo), float(w_hi)))
            p_ref[rows, lanes] = total / count - cur
        for lt in range(D_CONV // LANES):
            lanes = slice(lt * LANES, (lt + 1) * LANES)
            acc = jnp.zeros((nc, LANES), F32) + cb_ref[:, lanes]
            for k in range(CONV_WIDTH):
                acc = acc + cw_ref[k:k + 1, lanes] * hext_ref[i + first_tap + k, :, lanes]
            c_ref[rows, lanes] = acc
        return carry

    lax.fori_loop(0, ts, one_step, 0)
    y_pool = jnp.dot(p_ref[...].astype(BF16), pw_ref[...], preferred_element_type=F32) * ps_ref[...]
    conv = c_ref[...]
    mu = jnp.mean(conv, axis=-1, keepdims=True)
    cen = conv - mu
    var = jnp.mean(cen * cen, axis=-1, keepdims=True)
    hn = cen * lax.rsqrt(var + EPS) * lg_ref[...] + lb_ref[...]
    return jnp.concatenate([y_pool, hn * jax.nn.sigmoid(hn)], axis=1).astype(BF16)


def _mix_attn_kernel(x_ref, y4_ref, u4_ref, up_ref, hc_ref, k_ref, v_ref,
                     pw_ref, ps_ref, cw_ref, cb_ref, lg_ref, lb_ref,
                     d_ref, wglut_ref, wmix_ref, ga_ref, wq_ref, wo_ref,
                     o_ref, uext_ref, hext_ref, p_ref, c_ref):
    _, ts, nc, _ = x_ref.shape
    t = up_ref.shape[1]
    j = pl.program_id(1)
    chunk = lax.broadcasted_iota(jnp.int32, (nc, 1), 0)

    @pl.when(j == 0)
    def _():
        _fill_history(uext_ref, up_ref, POOL_PAD, chunk)
        _fill_history(hext_ref, hc_ref, CONV_PAD, chunk)

    y_pc = _pool_conv_tile(uext_ref, hext_ref, p_ref, c_ref, pw_ref, ps_ref, cw_ref, cb_ref, lg_ref, lb_ref,
                           j * ts, ts, t)

    yt = jnp.concatenate([y4_ref[:, s].reshape(D_SSM, nc) for s in range(ts)], axis=1)
    ut = jnp.concatenate([u4_ref[:, s].reshape(D_SSM, nc) for s in range(ts)], axis=1)
    yt = jax.nn.gelu(yt + d_ref[...] * ut, approximate=True)
    gate = jax.nn.sigmoid(jnp.dot(wglut_ref[...], yt.astype(BF16), preferred_element_type=F32))
    y_ssm = (yt * gate).T.astype(BF16)

    x = x_ref[0].reshape(ts * nc, D_MODEL)
    y_mix = jnp.concatenate([y_ssm, y_pc], axis=1)
    x = x + jnp.dot(y_mix, wmix_ref[...], preferred_element_type=F32)

    h = _rms(x, ga_ref[...]).astype(BF16)
    q = jnp.dot(h, wq_ref[...], preferred_element_type=F32) * (XHEAD_DIM ** -0.5)
    heads = []
    for hd in range(N_XHEADS):
        sl = slice(hd * XHEAD_DIM, (hd + 1) * XHEAD_DIM)
        sc = lax.dot_general(q[:, sl].astype(BF16), k_ref[0, :, sl], (((1,), (1,)), ((), ())),
                             preferred_element_type=F32)
        pr = jnp.exp(sc - jnp.max(sc, axis=-1, keepdims=True))
        l = jnp.sum(pr, axis=-1, keepdims=True)
        heads.append(jnp.dot(pr.astype(BF16), v_ref[0, :, sl], preferred_element_type=F32) / l)
    o = jnp.concatenate(heads, axis=-1).astype(BF16)
    x = x + jnp.dot(o, wo_ref[...], preferred_element_type=F32)
    o_ref[0] = x.reshape(ts, nc, D_MODEL)


def _mix_attn(x4, y4, u4, up4, hc4, k3d, v3d, pool_w_bd, pool_scale, conv_w, conv_b, ln_g, ln_b,
              d_col, w_glu_t, w_mix, ga, wq, wo, *, layer, ts):
    b, t, nc, _ = x4.shape
    xspec = pl.BlockSpec((1, ts, nc, D_MODEL), lambda i, j: (i, j, 0, 0))
    chan = pl.BlockSpec((N_SSM_GROUPS, ts, SSM_GROUP, nc), lambda i, j: (0, j, 0, i))
    whole = lambda w: pl.BlockSpec((1, t, nc, w), lambda i, j: (i, 0, 0, 0))
    mem = pl.BlockSpec((1, MEM_LEN, D_MODEL), lambda i, j: (i, 0, 0))
    return pl.pallas_call(
        _mix_attn_kernel,
        out_shape=jax.ShapeDtypeStruct((b, t, nc, D_MODEL), F32),
        grid=(b, t // ts),
        in_specs=[
            xspec, chan, chan, whole(D_POOL), whole(D_CONV), mem, mem,
            _const_spec((D_POOL, D_POOL)), _const_spec((1, D_POOL)),
            _const_spec((CONV_WIDTH, D_CONV)), _const_spec((1, D_CONV)),
            _const_spec((1, D_CONV)), _const_spec((1, D_CONV)),
            _const_spec((D_SSM, 1)), _const_spec((D_SSM, D_SSM)),
            _layer_spec((D_MODEL, D_MODEL), layer),
            _const_spec((1, D_MODEL)), _layer_spec((D_MODEL, D_MODEL), layer),
            _layer_spec((D_MODEL, D_MODEL), layer),
        ],
        out_specs=xspec,
        scratch_shapes=[pltpu.VMEM((POOL_PAD + t, nc, D_POOL), F32),
                        pltpu.VMEM((CONV_PAD + t, nc, D_CONV), F32),
                        pltpu.VMEM((ts * nc, D_POOL), F32),
                        pltpu.VMEM((ts * nc, D_CONV), F32)],
        compiler_params=_cparams("parallel", "arbitrary"),
    )(x4, y4, u4, up4, hc4, k3d, v3d, pool_w_bd, pool_scale, conv_w, conv_b, ln_g, ln_b,
      d_col, w_glu_t, w_mix, ga, wq, wo)


def _ffn_kernel(x_ref, g_ref, wg_ref, wu_ref, wd_ref, o_ref):
    o_ref[...] = _ffn_body(x_ref[...], g_ref, wg_ref, wu_ref, wd_ref)


def _ffn(x2d, norm_g, wg, wu, wd, *, layer, tm):
    n = x2d.shape[0]
    row = pl.BlockSpec((tm, D_MODEL), lambda i: (i, 0))
    return pl.pallas_call(
        _ffn_kernel,
        out_shape=jax.ShapeDtypeStruct((n, D_MODEL), F32),
        grid=(n // tm,),
        in_specs=[row, _const_spec((1, D_MODEL)), _layer_spec((D_MODEL, D_FF), layer),
                  _layer_spec((D_MODEL, D_FF), layer), _layer_spec((D_FF, D_MODEL), layer)],
        out_specs=row,
        compiler_params=_cparams("parallel"),
    )(x2d, norm_g, wg, wu, wd)


def _ffn_out_kernel(x_ref, g_ref, wg_ref, wu_ref, wd_ref, fg_ref, o_ref):
    _, ts, nc, _ = x_ref.shape
    y = _ffn_body(x_ref[0].reshape(ts * nc, D_MODEL), g_ref, wg_ref, wu_ref, wd_ref)
    y = _rms(y, fg_ref[...]).reshape(ts, nc, D_MODEL)
    o_ref[0] = jnp.swapaxes(y, 0, 1)


def _ffn_out(x4, norm_g, wg, wu, wd, final_g, *, layer, ts):
    b, t, nc, _ = x4.shape
    return pl.pallas_call(
        _ffn_out_kernel,
        out_shape=jax.ShapeDtypeStruct((b, nc, t, D_MODEL), F32),
        grid=(b, t // ts),
        in_specs=[pl.BlockSpec((1, ts, nc, D_MODEL), lambda i, j: (i, j, 0, 0)),
                  _const_spec((1, D_MODEL)), _layer_spec((D_MODEL, D_FF), layer),
                  _layer_spec((D_MODEL, D_FF), layer), _layer_spec((D_FF, D_MODEL), layer),
                  _const_spec((1, D_MODEL))],
        out_specs=pl.BlockSpec((1, nc, ts, D_MODEL), lambda i, j: (i, 0, j, 0)),
        compiler_params=_cparams("parallel", "parallel"),
    )(x4, norm_g, wg, wu, wd, final_g)


def _tile(n, pref):
    t = max(1, min(n, pref))
    assert n % t == 0, (n, t)
    return t


def kernel(x, mem, ffn1_norm, ffn1_w_gate, ffn1_w_up, ffn1_w_down, mix_norm, w_in, w_out, ssm_lambda_re, ssm_lambda_im, ssm_log_dt, ssm_b_re, ssm_b_im, ssm_c_re, ssm_c_im, ssm_d, ssm_w_glu, pool_w, pool_scale, conv_w, conv_b, conv_ln_g, conv_ln_b, xattn_norm, mem_norm, xattn_wq, xattn_wk, xattn_wv, xattn_wo, ffn2_norm, ffn2_w_gate, ffn2_w_up, ffn2_w_down, final_norm):
    bsz, seq, _ = x.shape
    depth = w_in.shape[0]
    n = bsz * seq
    assert seq % S5_CHUNK == 0
    n_chunks = seq // S5_CHUNK
    assert n_chunks & (n_chunks - 1) == 0 and n_chunks <= (1 << S5_MAX_SCAN_STEPS)
    bg = _tile(bsz, FFN_TOKEN_TILE // n_chunks)
    ts = _tile(S5_CHUNK, TOKEN_TILE // n_chunks)
    to = _tile(S5_CHUNK, SUBLANES)
    tm = _tile(n, FFN_TOKEN_TILE)
    tkv = _tile(bsz * MEM_LEN, TOKEN_TILE)
    bf = lambda w: w.astype(BF16)
    row = lambda v: v.reshape(1, -1)

    mem2d = mem.reshape(bsz * MEM_LEN, D_MODEL)
    f1g, f1u, f1d = bf(ffn1_w_gate), bf(ffn1_w_up), bf(ffn1_w_down)
    f2g, f2u, f2d = bf(ffn2_w_gate), bf(ffn2_w_up), bf(ffn2_w_down)
    wq, wk, wv, wo = bf(xattn_wq), bf(xattn_wk), bf(xattn_wv), bf(xattn_wo)
    w_mix = bf(w_out)
    for l in range(depth):
        in_args = (row(ffn1_norm[l]), f1g, f1u, f1d, row(mix_norm[l]), bf(w_in[l, :, :D_SSM].T),
                   bf(w_in[l, :, D_SSM:]))
        if l == 0:
            x4, u4, up4, hc4 = _ffn_mix_in_tm(x.reshape(bsz, n_chunks, S5_CHUNK, D_MODEL), *in_args,
                                              layer=l, ts=to)
        else:
            x4, u4, up4, hc4 = _ffn_mix_in(x4, *in_args, layer=l, bg=bg)

        zt, wst_t, gt, cp = _s5_prep(ssm_lambda_re[l], ssm_lambda_im[l], ssm_log_dt[l], ssm_b_re[l],
                                     ssm_b_im[l], ssm_c_re[l], ssm_c_im[l])
        y4 = _s5_main(u4, jnp.swapaxes(zt, 1, 2), jnp.swapaxes(wst_t, 1, 2), jnp.swapaxes(gt, 1, 2), cp,
                      n_chunks=n_chunks)

        k2d, v2d = _kv(mem2d, row(mem_norm[l]), wk, wv, layer=l, tm=tkv)
        pool_bd = jax.scipy.linalg.block_diag(*[pool_w[l, g] for g in range(len(POOL_WINDOWS))])
        x4 = _mix_attn(x4, y4, u4, up4, hc4, k2d.reshape(bsz, MEM_LEN, D_MODEL),
                       v2d.reshape(bsz, MEM_LEN, D_MODEL), bf(pool_bd), row(pool_scale[l]), conv_w[l],
                       row(conv_b[l]), row(conv_ln_g[l]), row(conv_ln_b[l]), ssm_d[l].reshape(D_SSM, 1),
                       bf(ssm_w_glu[l].T), w_mix,
                       row(xattn_norm[l]), wq, wo, layer=l, ts=ts)

        if l == depth - 1:
            out = _ffn_out(x4, row(ffn2_norm[l]), f2g, f2u, f2d, row(final_norm), layer=l, ts=to)
            return out.reshape(bsz, seq, D_MODEL)
        x2d = _ffn(x4.reshape(n, D_MODEL), row(ffn2_norm[l]), f2g, f2u, f2d, layer=l, tm=tm)
        x4 = x2d.reshape(bsz, S5_CHUNK, n_chunks, D_MODEL)
```

```python
import functools

import jax
import jax.numpy as jnp
from jax import lax
from jax.experimental import pallas as pl
from jax.experimental.pallas import tpu as pltpu

F32 = jnp.float32
BF16 = jnp.bfloat16

D_MODEL = 1024
MEM_LEN = 256
D_SSM = 384
D_POOL = 256
D_CONV = 384
D_PC = D_POOL + D_CONV
SSM_GROUP = 16
N_SSM_GROUPS = 24
SSM_STATE = 64
POOL_WINDOWS = (2, 4, 8, 16)
POOL_GROUP = 64
CONV_WIDTH = 31
D_IN = D_SSM + D_POOL + 2 * D_CONV
D_FF = 2816
N_XHEADS = 4
XHEAD_DIM = 256
EPS = 1e-6

SUBLANES = 8
LANES = 128
MXU_TILE = 256

TOKEN_TILE = 512
FFN_TOKEN_TILE = 1024
FF_CHUNK = MXU_TILE
S5_SUB = MXU_TILE // SSM_GROUP
S5_CHUNK = S5_SUB
S5_NB = S5_CHUNK // S5_SUB
S5_ROWS = S5_CHUNK * SSM_GROUP
S5_MAX_SCAN_STEPS = 7
CONV_PAD = 32
POOL_PAD = 16
VMEM_LIMIT = 56 * 1024 * 1024


def _cparams(*sem):
    return pltpu.CompilerParams(dimension_semantics=sem, vmem_limit_bytes=VMEM_LIMIT)


def _const_spec(shape):
    nd = len(shape)
    return pl.BlockSpec(shape, lambda *_: (0,) * nd, pipeline_mode=pl.Buffered(1))


def _layer_spec(shape, layer):
    nd = len(shape)
    return pl.BlockSpec((None,) + shape, lambda *_: (layer,) + (0,) * nd, pipeline_mode=pl.Buffered(1))


def _rms(x, g):
    ms = jnp.mean(x * x, axis=-1, keepdims=True)
    return x * lax.rsqrt(ms + EPS) * g


def _ffn_body(x, g_ref, wg_ref, wu_ref, wd_ref):
    h = _rms(x, g_ref[...]).astype(BF16)
    acc = jnp.zeros(x.shape, F32)
    for c in range(D_FF // FF_CHUNK):
        sl = slice(c * FF_CHUNK, (c + 1) * FF_CHUNK)
        gate = jnp.dot(h, wg_ref[:, sl], preferred_element_type=F32)
        up = jnp.dot(h, wu_ref[:, sl], preferred_element_type=F32)
        act = (gate * jax.nn.sigmoid(gate) * up).astype(BF16)
        acc = acc + jnp.dot(act, wd_ref[sl, :], preferred_element_type=F32)
    return x + 0.5 * acc


def _ffn_mix_in_kernel(x_ref, g1_ref, wg_ref, wu_ref, wd_ref, g2_ref, wst_ref, wrest_ref,
                       xo_ref, u4_ref, up_ref, hc_ref):
    b, nc, _ = x_ref.shape
    y = _ffn_body(x_ref[...].reshape(b * nc, D_MODEL), g1_ref, wg_ref, wu_ref, wd_ref)
    xo_ref[...] = y.reshape(b, nc, D_MODEL)
    h = _rms(y, g2_ref[...]).astype(BF16)
    ut = lax.dot_general(wst_ref[...], h, (((1,), (1,)), ((), ())), preferred_element_type=F32)
    u4_ref[...] = ut.reshape(N_SSM_GROUPS, SSM_GROUP, b * nc)
    z = jnp.dot(h, wrest_ref[...], preferred_element_type=F32)
    up_ref[...] = z[:, :D_POOL].reshape(b, nc, D_POOL)
    v = z[:, D_POOL:D_POOL + D_CONV]
    g = z[:, D_POOL + D_CONV:]
    hc_ref[...] = (v * jax.nn.sigmoid(g)).reshape(b, nc, D_CONV)


def _ffn_mix_in_tm_kernel(x_ref, g1_ref, wg_ref, wu_ref, wd_ref, g2_ref, wst_ref, wrest_ref,
                          xo_ref, u4_ref, up_ref, hc_ref):
    _, nc, ts, _ = x_ref.shape
    x = jnp.swapaxes(x_ref[0], 0, 1).reshape(ts * nc, D_MODEL)
    y = _ffn_body(x, g1_ref, wg_ref, wu_ref, wd_ref)
    xo_ref[0] = y.reshape(ts, nc, D_MODEL)
    h = _rms(y, g2_ref[...]).astype(BF16)
    ut = lax.dot_general(wst_ref[...], h, (((1,), (1,)), ((), ())), preferred_element_type=F32)
    for s in range(ts):
        u4_ref[:, s] = ut[:, s * nc:(s + 1) * nc].reshape(N_SSM_GROUPS, SSM_GROUP, nc)
    z = jnp.dot(h, wrest_ref[...], preferred_element_type=F32)
    up_ref[0] = z[:, :D_POOL].reshape(ts, nc, D_POOL)
    v = z[:, D_POOL:D_POOL + D_CONV]
    g = z[:, D_POOL + D_CONV:]
    hc_ref[0] = (v * jax.nn.sigmoid(g)).reshape(ts, nc, D_CONV)


def _ffn_mix_in_tm(x_tm, g1, wg, wu, wd, g2, w_ssm_t, w_rest, *, layer, ts):
    b, nc, t, _ = x_tm.shape
    tile = lambda w: pl.BlockSpec((1, ts, nc, w), lambda i, j: (i, j, 0, 0))
    return pl.pallas_call(
        _ffn_mix_in_tm_kernel,
        out_shape=(jax.ShapeDtypeStruct((b, t, nc, D_MODEL), F32),
                   jax.ShapeDtypeStruct((N_SSM_GROUPS, t, SSM_GROUP, b * nc), F32),
                   jax.ShapeDtypeStruct((b, t, nc, D_POOL), F32),
                   jax.ShapeDtypeStruct((b, t, nc, D_CONV), F32)),
        grid=(b, t // ts),
        in_specs=[
            pl.BlockSpec((1, nc, ts, D_MODEL), lambda i, j: (i, 0, j, 0)),
            _const_spec((1, D_MODEL)),
            _layer_spec((D_MODEL, D_FF), layer),
            _layer_spec((D_MODEL, D_FF), layer),
            _layer_spec((D_FF, D_MODEL), layer),
            _const_spec((1, D_MODEL)),
            _const_spec((D_SSM, D_MODEL)),
            _const_spec((D_MODEL, D_IN - D_SSM)),
        ],
        out_specs=(tile(D_MODEL),
                   pl.BlockSpec((N_SSM_GROUPS, ts, SSM_GROUP, nc), lambda i, j: (0, j, 0, i)),
                   tile(D_POOL), tile(D_CONV)),
        compiler_params=_cparams("parallel", "parallel"),
    )(x_tm, g1, wg, wu, wd, g2, w_ssm_t, w_rest)


def _ffn_mix_in(x4, g1, wg, wu, wd, g2, w_ssm_t, w_rest, *, layer, bg):
    b, t, nc, _ = x4.shape
    step = lambda w: pl.BlockSpec((bg, None, nc, w), lambda g, i: (g, i, 0, 0))
    return pl.pallas_call(
        _ffn_mix_in_kernel,
        out_shape=(jax.ShapeDtypeStruct((b, t, nc, D_MODEL), F32),
                   jax.ShapeDtypeStruct((N_SSM_GROUPS, t, SSM_GROUP, b * nc), F32),
                   jax.ShapeDtypeStruct((b, t, nc, D_POOL), F32),
                   jax.ShapeDtypeStruct((b, t, nc, D_CONV), F32)),
        grid=(b // bg, t),
        in_specs=[
            step(D_MODEL),
            _const_spec((1, D_MODEL)),
            _layer_spec((D_MODEL, D_FF), layer),
            _layer_spec((D_MODEL, D_FF), layer),
            _layer_spec((D_FF, D_MODEL), layer),
            _const_spec((1, D_MODEL)),
            _const_spec((D_SSM, D_MODEL)),
            _const_spec((D_MODEL, D_IN - D_SSM)),
        ],
        out_specs=(step(D_MODEL),
                   pl.BlockSpec((N_SSM_GROUPS, None, SSM_GROUP, bg * nc), lambda g, i: (0, i, 0, g)),
                   step(D_POOL), step(D_CONV)),
        compiler_params=_cparams("parallel", "parallel"),
    )(x4, g1, wg, wu, wd, g2, w_ssm_t, w_rest)


def _s5_prep_kernel(lr2_ref, li2_ref, ldt_ref, br2_ref, bi2_ref, lrc_ref, lic_ref, crt_ref, cit_ref,
                    zt_ref, wst_ref, gt_ref, cp_ref):
    hi = lax.Precision.HIGHEST
    lane = lax.broadcasted_iota(jnp.int32, (1, LANES), 1)
    first = lane < SSM_STATE

    lr2 = lr2_ref[0]
    li2 = li2_ref[0]
    dt = jnp.exp(ldt_ref[0])
    rho = lr2 * dt
    th = li2 * dt

    def pw_row(e):
        mag = jnp.exp(e * rho)
        return mag * jnp.cos(e * th), mag * jnp.sin(e * th)

    one = jnp.ones((1, 1), F32)
    a_r, a_i = pw_row(one)
    den = lr2 * lr2 + li2 * li2
    z_r = ((a_r - 1.0) * lr2 + a_i * li2) / den
    z_i = (a_i * lr2 - (a_r - 1.0) * li2) / den
    br2 = br2_ref[0]
    bi2 = bi2_ref[0]
    b1 = jnp.where(first, br2, bi2)
    b2 = jnp.where(first, -bi2, br2)
    y1 = z_r * b1 + z_i * b2
    y2 = z_r * b2 - z_i * b1

    e_lo = (S5_SUB - 1 - lax.broadcasted_iota(jnp.int32, (S5_SUB, 1), 0)).astype(F32)
    lo_r, lo_i = pw_row(e_lo)
    lob1 = jnp.concatenate([lo_r[i:i + 1] * y1 + lo_i[i:i + 1] * y2 for i in range(S5_SUB)], axis=0)
    lob2 = jnp.concatenate([lo_r[i:i + 1] * y2 - lo_i[i:i + 1] * y1 for i in range(S5_SUB)], axis=0)
    e_hi = (S5_SUB * (S5_NB - 1 - lax.broadcasted_iota(jnp.int32, (S5_NB, 1), 0))).astype(F32)
    hi_r, hi_i = pw_row(e_hi)
    pb = jnp.concatenate([hi_r[m:m + 1] * lob1 + hi_i[m:m + 1] * lob2 for m in range(S5_NB)], axis=0)
    wst_ref[0] = pb.astype(BF16)

    lrc = lrc_ref[0]
    lic = lic_ref[0]
    rho_c = lrc * dt
    th_c = lic * dt
    mag_c = jnp.exp(rho_c)
    ac_r = mag_c * jnp.cos(th_c)
    ac_i = mag_c * jnp.sin(th_c)
    rc_r, rc_i = [], []
    for half in range(MXU_TILE // LANES):
        e = (lane // SSM_GROUP + half * (LANES // SSM_GROUP)).astype(F32)
        mag = jnp.exp(e * rho_c)
        t_r = mag * jnp.cos(e * th_c)
        t_i = mag * jnp.sin(e * th_c)
        c_r = crt_ref[0, :, half * LANES:(half + 1) * LANES]
        c_i = cit_ref[0, :, half * LANES:(half + 1) * LANES]
        rc_r.append(t_r * c_r - t_i * c_i)
        rc_i.append(t_r * c_i + t_i * c_r)
    ct_stack = jnp.concatenate([crt_ref[0], -cit_ref[0]], axis=0)

    def csq(v_r, v_i):
        return v_r * v_r - v_i * v_i, 2.0 * v_r * v_i

    a16_r, a16_i = ac_r, ac_i
    for _ in range(S5_SUB.bit_length() - 1):
        a16_r, a16_i = csq(a16_r, a16_i)
    g_r, g_i = ac_r, ac_i
    for j in range(S5_NB):
        for half in range(MXU_TILE // LANES):
            lo = j * MXU_TILE + half * LANES
            gt_ref[0, :SSM_STATE, lo:lo + LANES] = (g_r * rc_r[half] - g_i * rc_i[half]).astype(BF16)
            gt_ref[0, SSM_STATE:, lo:lo + LANES] = (-(g_r * rc_i[half] + g_i * rc_r[half])).astype(BF16)
        g_r, g_i = g_r * a16_r - g_i * a16_i, g_r * a16_i + g_i * a16_r

    w_r, w_i = a16_r, a16_i
    for _ in range(S5_NB.bit_length() - 1):
        w_r, w_i = csq(w_r, w_i)
    for r in range(S5_MAX_SCAN_STEPS):
        cp_ref[0, 2 * r * SSM_STATE:(2 * r + 1) * SSM_STATE, :] = w_r
        cp_ref[0, (2 * r + 1) * SSM_STATE:(2 * r + 2) * SSM_STATE, :] = w_i
        w_r, w_i = csq(w_r, w_i)

    off_rows = (S5_CHUNK - S5_SUB) * SSM_GROUP
    if off_rows:
        rc_stack = jnp.concatenate([jnp.concatenate(rc_r, axis=1), -jnp.concatenate(rc_i, axis=1)], axis=0)
        lo_row = (S5_SUB - 1) * SSM_GROUP
        zt_off = jnp.dot(pb[lo_row:lo_row + off_rows], rc_stack, precision=hi, preferred_element_type=F32)
        zt_ref[0, :off_rows, :] = zt_off.astype(BF16)
    kall = jnp.dot(pb[off_rows:], ct_stack, precision=hi, preferred_element_type=F32)
    colblk = lax.broadcasted_iota(jnp.int32, (1, MXU_TILE), 1) // SSM_GROUP
    zd = jnp.zeros((MXU_TILE, MXU_TILE), F32)
    for jj in range(S5_SUB):
        sh = (S5_SUB - 1 - jj) * SSM_GROUP
        shifted = kall if sh == 0 else jnp.concatenate(
            [kall[sh:], jnp.zeros((sh, MXU_TILE), F32)], axis=0)
        zd = jnp.where(colblk == jj, shifted, zd)
    zt_ref[0, off_rows:, :] = zd.astype(BF16)


def _s5_prep(lam_re, lam_im, log_dt, b_re, b_im, c_re, c_im):
    g = N_SSM_GROUPS
    lr2 = jnp.tile(lam_re, (1, 2))[:, None, :]
    li2 = jnp.tile(lam_im, (1, 2))[:, None, :]
    ldt = jnp.broadcast_to(log_dt[:, None, None], (g, 1, LANES))
    br2 = jnp.tile(jnp.swapaxes(b_re, 1, 2), (1, 1, 2))
    bi2 = jnp.tile(jnp.swapaxes(b_im, 1, 2), (1, 1, 2))
    lrc = jnp.broadcast_to(lam_re[:, :, None], (g, SSM_STATE, LANES))
    lic = jnp.broadcast_to(lam_im[:, :, None], (g, SSM_STATE, LANES))
    crt = jnp.tile(jnp.swapaxes(c_re, 1, 2), (1, 1, S5_SUB))
    cit = jnp.tile(jnp.swapaxes(c_im, 1, 2), (1, 1, S5_SUB))

    def spec(*shape):
        return pl.BlockSpec((1,) + shape, lambda i: (i, 0, 0))

    cp_rows = 2 * S5_MAX_SCAN_STEPS * SSM_STATE
    return pl.pallas_call(
        _s5_prep_kernel,
        out_shape=(jax.ShapeDtypeStruct((g, S5_ROWS, MXU_TILE), BF16),
                   jax.ShapeDtypeStruct((g, S5_ROWS, LANES), BF16),
                   jax.ShapeDtypeStruct((g, LANES, S5_ROWS), BF16),
                   jax.ShapeDtypeStruct((g, cp_rows, LANES), F32)),
        grid=(g,),
        in_specs=[spec(1, LANES), spec(1, LANES), spec(1, LANES),
                  spec(SSM_GROUP, LANES), spec(SSM_GROUP, LANES),
                  spec(SSM_STATE, LANES), spec(SSM_STATE, LANES),
                  spec(SSM_STATE, MXU_TILE), spec(SSM_STATE, MXU_TILE)],
        out_specs=(spec(S5_ROWS, MXU_TILE), spec(S5_ROWS, LANES), spec(LANES, S5_ROWS),
                   spec(cp_rows, LANES)),
        compiler_params=_cparams("parallel"),
    )(lr2, li2, ldt, br2, bi2, lrc, lic, crt, cit)


def _s5_main_kernel(u_ref, zw_ref, wst_ref, g_ref, cp_ref, d_ref, o_ref, *, n_chunks):
    ncols = u_ref.shape[-1]
    u = u_ref[0].reshape(S5_ROWS, ncols).astype(BF16)
    state = jnp.dot(wst_ref[0], u, preferred_element_type=F32)
    s_r, s_i = state[:SSM_STATE], state[SSM_STATE:]
    chunk = lax.broadcasted_iota(jnp.int32, (1, ncols), 1) % n_chunks
    reps = ncols // LANES if ncols > LANES else 1

    def wide(v):
        return jnp.concatenate([v] * reps, axis=1)[:, :ncols]

    r = 0
    while (1 << r) < n_chunks:
        s = 1 << r
        p_r = jnp.where(chunk >= s, pltpu.roll(s_r, s, 1), 0.0)
        p_i = jnp.where(chunk >= s, pltpu.roll(s_i, s, 1), 0.0)
        w_r = wide(cp_ref[0, 2 * r * SSM_STATE:(2 * r + 1) * SSM_STATE, :])
        w_i = wide(cp_ref[0, (2 * r + 1) * SSM_STATE:(2 * r + 2) * SSM_STATE, :])
        s_r, s_i = s_r + w_r * p_r - w_i * p_i, s_i + w_r * p_i + w_i * p_r
        r += 1
    c_r = jnp.where(chunk >= 1, pltpu.roll(s_r, 1, 1), 0.0)
    c_i = jnp.where(chunk >= 1, pltpu.roll(s_i, 1, 1), 0.0)
    carry = jnp.concatenate([c_r, c_i], axis=0).astype(BF16)
    for j in range(S5_NB):
        k = (j + 1) * MXU_TILE
        y = jnp.dot(zw_ref[0, :, (S5_NB - 1 - j) * MXU_TILE:], u[:k], preferred_element_type=F32)
        y = y + jnp.dot(g_ref[0, j * MXU_TILE:(j + 1) * MXU_TILE, :], carry, preferred_element_type=F32)
        rows = slice(j * MXU_TILE, (j + 1) * MXU_TILE)
        y = y + d_ref[0, rows] * u_ref[0].reshape(S5_ROWS, ncols)[rows]
        o_ref[0, j * S5_SUB:(j + 1) * S5_SUB] = y.reshape(S5_SUB, SSM_GROUP, ncols)


def _s5_main(u4, zw, wst, gm, cp, d_rows, *, n_chunks):
    g, t, _, ncols = u4.shape
    cp_rows = cp.shape[1]

    def spec(*shape):
        return pl.BlockSpec((1,) + shape, lambda i: (i,) + (0,) * len(shape))

    return pl.pallas_call(
        functools.partial(_s5_main_kernel, n_chunks=n_chunks),
        out_shape=jax.ShapeDtypeStruct((g, t, SSM_GROUP, ncols), F32),
        grid=(g,),
        in_specs=[spec(t, SSM_GROUP, ncols), spec(MXU_TILE, S5_ROWS), spec(LANES, S5_ROWS),
                  spec(S5_ROWS, LANES), spec(cp_rows, LANES), spec(S5_ROWS, 1)],
        out_specs=spec(t, SSM_GROUP, ncols),
        compiler_params=_cparams("parallel"),
    )(u4, zw, wst, gm, cp, d_rows)


def _kv_kernel(m_ref, g_ref, wk_ref, wv_ref, k_ref, v_ref):
    m = _rms(m_ref[...], g_ref[...]).astype(BF16)
    k_ref[...] = jnp.dot(m, wk_ref[...], preferred_element_type=F32).astype(BF16)
    v_ref[...] = jnp.dot(m, wv_ref[...], preferred_element_type=F32).astype(BF16)


def _kv(mem2d, norm_g, wk, wv, *, layer, tm):
    n = mem2d.shape[0]
    row = pl.BlockSpec((tm, D_MODEL), lambda i: (i, 0))
    return pl.pallas_call(
        _kv_kernel,
        out_shape=(jax.ShapeDtypeStruct((n, D_MODEL), BF16), jax.ShapeDtypeStruct((n, D_MODEL), BF16)),
        grid=(n // tm,),
        in_specs=[row, _const_spec((1, D_MODEL)), _layer_spec((D_MODEL, D_MODEL), layer),
                  _layer_spec((D_MODEL, D_MODEL), layer)],
        out_specs=(row, row),
        compiler_params=_cparams("parallel"),
    )(mem2d, norm_g, wk, wv)


def _fill_history(ext_ref, src_ref, pad, chunk):
    t = src_ref.shape[1]
    ext_ref[pad:] = src_ref[0]
    for e in range(pad):
        back = e // t + 1
        v = src_ref[0, t - 1 - e % t]
        ext_ref[pad - 1 - e] = jnp.where(chunk >= back, pltpu.roll(v, back, 0), 0.0)


def _pool_conv_tile(uext_ref, hext_ref, p_ref, c_ref, pw_ref, ps_ref, cw_ref, cb_ref, lg_ref, lb_ref,
                    step0, ts, t):
    nc = uext_ref.shape[1]
    chunk = lax.broadcasted_iota(jnp.int32, (nc, 1), 0)
    low = lax.broadcasted_iota(jnp.int32, (1, LANES), 1) < POOL_GROUP
    first_tap = CONV_PAD - (CONV_WIDTH - 1)
    groups_per_tile = LANES // POOL_GROUP

    def one_step(s, carry):
        i = step0 + s
        rows = pl.ds(pl.multiple_of(s * nc, nc), nc)
        pos = (chunk * t + i + 1).astype(F32)
        for lt in range(D_POOL // LANES):
            lanes = slice(lt * LANES, (lt + 1) * LANES)
            w_lo, w_hi = POOL_WINDOWS[groups_per_tile * lt:groups_per_tile * (lt + 1)]
            cur = uext_ref[POOL_PAD + i, :, lanes]
            run = cur
            saved = {}
            for d in range(1, w_hi):
                run = run + uext_ref[POOL_PAD + i - d, :, lanes]
                if d + 1 in (w_lo, w_hi):
                    saved[d + 1] = run
            total = jnp.where(low, saved[w_lo], saved[w_hi])
            count = jnp.minimum(pos, jnp.where(low, float(w_lo), float(w_hi)))
            p_ref[rows, lanes] = total / count - cur
        for lt in range(D_CONV // LANES):
            lanes = slice(lt * LANES, (lt + 1) * LANES)
            acc = jnp.zeros((nc, LANES), F32) + cb_ref[:, lanes]
            for k in range(CONV_WIDTH):
                acc = acc + cw_ref[k:k + 1, lanes] * hext_ref[i + first_tap + k, :, lanes]
            c_ref[rows, lanes] = acc
        return carry

    lax.fori_loop(0, ts, one_step, 0)
    y_pool = jnp.dot(p_ref[...].astype(BF16), pw_ref[...], preferred_element_type=F32) * ps_ref[...]
    conv = c_ref[...]
    mu = jnp.mean(conv, axis=-1, keepdims=True)
    cen = conv - mu
    var = jnp.mean(cen * cen, axis=-1, keepdims=True)
    hn = cen * lax.rsqrt(var + EPS) * lg_ref[...] + lb_ref[...]
    return jnp.concatenate([y_pool, hn * jax.nn.sigmoid(hn)], axis=1).astype(BF16)


def _mix_attn_kernel(x_ref, y4_ref, up_ref, hc_ref, k_ref, v_ref,
                     pw_ref, ps_ref, cw_ref, cb_ref, lg_ref, lb_ref,
                     wglut_ref, wmix_ref, ga_ref, wq_ref, wo_ref,
                     o_ref, uext_ref, hext_ref, p_ref, c_ref):
    _, ts, nc, _ = x_ref.shape
    t = up_ref.shape[1]
    j = pl.program_id(1)
    chunk = lax.broadcasted_iota(jnp.int32, (nc, 1), 0)

    @pl.when(j == 0)
    def _():
        _fill_history(uext_ref, up_ref, POOL_PAD, chunk)
        _fill_history(hext_ref, hc_ref, CONV_PAD, chunk)

    y_pc = _pool_conv_tile(uext_ref, hext_ref, p_ref, c_ref, pw_ref, ps_ref, cw_ref, cb_ref, lg_ref, lb_ref,
                           j * ts, ts, t)

    yt = jnp.concatenate([y4_ref[:, s].reshape(D_SSM, nc) for s in range(ts)], axis=1)
    yt = jax.nn.gelu(yt, approximate=True)
    gate = jax.nn.sigmoid(jnp.dot(wglut_ref[...], yt.astype(BF16), preferred_element_type=F32))
    y_ssm = (yt * gate).T.astype(BF16)

    x = x_ref[0].reshape(ts * nc, D_MODEL)
    y_mix = jnp.concatenate([y_ssm, y_pc], axis=1)
    x = x + jnp.dot(y_mix, wmix_ref[...], preferred_element_type=F32)

    h = _rms(x, ga_ref[...]).astype(BF16)
    q = jnp.dot(h, wq_ref[...], preferred_element_type=F32) * (XHEAD_DIM ** -0.5)
    heads = []
    for hd in range(N_XHEADS):
        sl = slice(hd * XHEAD_DIM, (hd + 1) * XHEAD_DIM)
        sc = lax.dot_general(q[:, sl].astype(BF16), k_ref[0, :, sl], (((1,), (1,)), ((), ())),
                             preferred_element_type=F32)
        pr = jnp.exp(sc - jnp.max(sc, axis=-1, keepdims=True))
        l = jnp.sum(pr, axis=-1, keepdims=True)
        heads.append(jnp.dot(pr.astype(BF16), v_ref[0, :, sl], preferred_element_type=F32) / l)
    o = jnp.concatenate(heads, axis=-1).astype(BF16)
    x = x + jnp.dot(o, wo_ref[...], preferred_element_type=F32)
    o_ref[0] = x.reshape(ts, nc, D_MODEL)


def _mix_attn(x4, y4, up4, hc4, k3d, v3d, pool_w_bd, pool_scale, conv_w, conv_b, ln_g, ln_b,
              w_glu_t, w_mix, ga, wq, wo, *, layer, ts):
    b, t, nc, _ = x4.shape
    xspec = pl.BlockSpec((1, ts, nc, D_MODEL), lambda i, j: (i, j, 0, 0))
    chan = pl.BlockSpec((N_SSM_GROUPS, ts, SSM_GROUP, nc), lambda i, j: (0, j, 0, i))
    whole = lambda w: pl.BlockSpec((1, t, nc, w), lambda i, j: (i, 0, 0, 0))
    mem = pl.BlockSpec((1, MEM_LEN, D_MODEL), lambda i, j: (i, 0, 0))
    return pl.pallas_call(
        _mix_attn_kernel,
        out_shape=jax.ShapeDtypeStruct((b, t, nc, D_MODEL), F32),
        grid=(b, t // ts),
        in_specs=[
            xspec, chan, whole(D_POOL), whole(D_CONV), mem, mem,
            _const_spec((D_POOL, D_POOL)), _const_spec((1, D_POOL)),
            _const_spec((CONV_WIDTH, D_CONV)), _const_spec((1, D_CONV)),
            _const_spec((1, D_CONV)), _const_spec((1, D_CONV)),
            _const_spec((D_SSM, D_SSM)),
            _layer_spec((D_MODEL, D_MODEL), layer),
            _const_spec((1, D_MODEL)), _layer_spec((D_MODEL, D_MODEL), layer),
            _layer_spec((D_MODEL, D_MODEL), layer),
        ],
        out_specs=xspec,
        scratch_shapes=[pltpu.VMEM((POOL_PAD + t, nc, D_POOL), F32),
                        pltpu.VMEM((CONV_PAD + t, nc, D_CONV), F32),
                        pltpu.VMEM((ts * nc, D_POOL), F32),
                        pltpu.VMEM((ts * nc, D_CONV), F32)],
        compiler_params=_cparams("parallel", "arbitrary"),
    )(x4, y4, up4, hc4, k3d, v3d, pool_w_bd, pool_scale, conv_w, conv_b, ln_g, ln_b,
      w_glu_t, w_mix, ga, wq, wo)


def _ffn_kernel(x_ref, g_ref, wg_ref, wu_ref, wd_ref, o_ref):
    o_ref[...] = _ffn_body(x_ref[...], g_ref, wg_ref, wu_ref, wd_ref)


def _ffn(x2d, norm_g, wg, wu, wd, *, layer, tm):
    n = x2d.shape[0]
    row = pl.BlockSpec((tm, D_MODEL), lambda i: (i, 0))
    return pl.pallas_call(
        _ffn_kernel,
        out_shape=jax.ShapeDtypeStruct((n, D_MODEL), F32),
        grid=(n // tm,),
        in_specs=[row, _const_spec((1, D_MODEL)), _layer_spec((D_MODEL, D_FF), layer),
                  _layer_spec((D_MODEL, D_FF), layer), _layer_spec((D_FF, D_MODEL), layer)],
        out_specs=row,
        compiler_params=_cparams("parallel"),
    )(x2d, norm_g, wg, wu, wd)


def _ffn_out_kernel(x_ref, g_ref, wg_ref, wu_ref, wd_ref, fg_ref, o_ref):
    _, ts, nc, _ = x_ref.shape
    y = _ffn_body(x_ref[0].reshape(ts * nc, D_MODEL), g_ref, wg_ref, wu_ref, wd_ref)
    y = _rms(y, fg_ref[...]).reshape(ts, nc, D_MODEL)
    o_ref[0] = jnp.swapaxes(y, 0, 1)


def _ffn_out(x4, norm_g, wg, wu, wd, final_g, *, layer, ts):
    b, t, nc, _ = x4.shape
    return pl.pallas_call(
        _ffn_out_kernel,
        out_shape=jax.ShapeDtypeStruct((b, nc, t, D_MODEL), F32),
        grid=(b, t // ts),
        in_specs=[pl.BlockSpec((1, ts, nc, D_MODEL), lambda i, j: (i, j, 0, 0)),
                  _const_spec((1, D_MODEL)), _layer_spec((D_MODEL, D_FF), layer),
                  _layer_spec((D_MODEL, D_FF), layer), _layer_spec((D_FF, D_MODEL), layer),
                  _const_spec((1, D_MODEL))],
        out_specs=pl.BlockSpec((1, nc, ts, D_MODEL), lambda i, j: (i, 0, j, 0)),
        compiler_params=_cparams("parallel", "parallel"),
    )(x4, norm_g, wg, wu, wd, final_g)


def _tile(n, pref):
    t = max(1, min(n, pref))
    assert n % t == 0, (n, t)
    return t


def kernel(x, mem, ffn1_norm, ffn1_w_gate, ffn1_w_up, ffn1_w_down, mix_norm, w_in, w_out, ssm_lambda_re, ssm_lambda_im, ssm_log_dt, ssm_b_re, ssm_b_im, ssm_c_re, ssm_c_im, ssm_d, ssm_w_glu, pool_w, pool_scale, conv_w, conv_b, conv_ln_g, conv_ln_b, xattn_norm, mem_norm, xattn_wq, xattn_wk, xattn_wv, xattn_wo, ffn2_norm, ffn2_w_gate, ffn2_w_up, ffn2_w_down, final_norm):
    bsz, seq, _ = x.shape
    depth = w_in.shape[0]
    n = bsz * seq
    assert seq % S5_CHUNK == 0
    n_chunks = seq // S5_CHUNK
    assert n_chunks & (n_chunks - 1) == 0 and n_chunks <= (1 << S5_MAX_SCAN_STEPS)
    bg = _tile(bsz, FFN_TOKEN_TILE // n_chunks)
    ts = _tile(S5_CHUNK, TOKEN_TILE // n_chunks)
    to = _tile(S5_CHUNK, SUBLANES)
    tm = _tile(n, FFN_TOKEN_TILE)
    tkv = _tile(bsz * MEM_LEN, TOKEN_TILE)
    bf = lambda w: w.astype(BF16)
    row = lambda v: v.reshape(1, -1)

    mem2d = mem.reshape(bsz * MEM_LEN, D_MODEL)
    f1g, f1u, f1d = bf(ffn1_w_gate), bf(ffn1_w_up), bf(ffn1_w_down)
    f2g, f2u, f2d = bf(ffn2_w_gate), bf(ffn2_w_up), bf(ffn2_w_down)
    wq, wk, wv, wo = bf(xattn_wq), bf(xattn_wk), bf(xattn_wv), bf(xattn_wo)
    w_mix = bf(w_out)
    for l in range(depth):
        in_args = (row(ffn1_norm[l]), f1g, f1u, f1d, row(mix_norm[l]), bf(w_in[l, :, :D_SSM].T),
                   bf(w_in[l, :, D_SSM:]))
        if l == 0:
            x4, u4, up4, hc4 = _ffn_mix_in_tm(x.reshape(bsz, n_chunks, S5_CHUNK, D_MODEL), *in_args,
                                              layer=l, ts=to)
        else:
            x4, u4, up4, hc4 = _ffn_mix_in(x4, *in_args, layer=l, bg=bg)

        zt, wst_t, gt, cp = _s5_prep(ssm_lambda_re[l], ssm_lambda_im[l], ssm_log_dt[l], ssm_b_re[l],
                                     ssm_b_im[l], ssm_c_re[l], ssm_c_im[l])
        d_rows = jnp.tile(ssm_d[l].reshape(N_SSM_GROUPS, 1, SSM_GROUP), (1, S5_CHUNK, 1))
        y4 = _s5_main(u4, jnp.swapaxes(zt, 1, 2), jnp.swapaxes(wst_t, 1, 2), jnp.swapaxes(gt, 1, 2), cp,
                      d_rows.reshape(N_SSM_GROUPS, S5_ROWS, 1), n_chunks=n_chunks)

        k2d, v2d = _kv(mem2d, row(mem_norm[l]), wk, wv, layer=l, tm=tkv)
        pool_bd = jax.scipy.linalg.block_diag(*[pool_w[l, g] for g in range(len(POOL_WINDOWS))])
        x4 = _mix_attn(x4, y4, up4, hc4, k2d.reshape(bsz, MEM_LEN, D_MODEL),
                       v2d.reshape(bsz, MEM_LEN, D_MODEL), bf(pool_bd), row(pool_scale[l]), conv_w[l],
                       row(conv_b[l]), row(conv_ln_g[l]), row(conv_ln_b[l]),
                       bf(ssm_w_glu[l].T), w_mix,
                       row(xattn_norm[l]), wq, wo, layer=l, ts=ts)

        if l == depth - 1:
            out = _ffn_out(x4, row(ffn2_norm[l]), f2g, f2u, f2d, row(final_norm), layer=l, ts=to)
            return out.reshape(bsz, seq, D_MODEL)
        x2d = _ffn(x4.reshape(n, D_MODEL), row(ffn2_norm[l]), f2g, f2u, f2d, layer=l, tm=tm)
        x4 = x2d.reshape(bsz, S5_CHUNK, n_chunks, D_MODEL)
```

```python
import functools

import jax
import jax.numpy as jnp
from jax import lax
from jax.experimental import pallas as pl
from jax.experimental.pallas import tpu as pltpu

F32 = jnp.float32
BF16 = jnp.bfloat16

D_MODEL = 1024
MEM_LEN = 256
D_SSM = 384
D_POOL = 256
D_CONV = 384
D_PC = D_POOL + D_CONV
SSM_GROUP = 16
N_SSM_GROUPS = 24
SSM_STATE = 64
POOL_WINDOWS = (2, 4, 8, 16)
POOL_GROUP = 64
CONV_WIDTH = 31
D_IN = D_SSM + D_POOL + 2 * D_CONV
D_FF = 2816
N_XHEADS = 4
XHEAD_DIM = 256
EPS = 1e-6

SUBLANES = 8
LANES = 128
MXU_TILE = 256

TOKEN_TILE = 512
FFN_TOKEN_TILE = 1024
FF_CHUNK = MXU_TILE
S5_SUB = MXU_TILE // SSM_GROUP
S5_CHUNK = S5_SUB
S5_NB = S5_CHUNK // S5_SUB
S5_ROWS = S5_CHUNK * SSM_GROUP
S5_MAX_SCAN_STEPS = 7
CONV_PAD = 32
POOL_PAD = 16
VMEM_LIMIT = 56 * 1024 * 1024


def _cparams(*sem):
    return pltpu.CompilerParams(dimension_semantics=sem, vmem_limit_bytes=VMEM_LIMIT)


def _const_spec(shape):
    nd = len(shape)
    return pl.BlockSpec(shape, lambda *_: (0,) * nd, pipeline_mode=pl.Buffered(1))


def _layer_spec(shape, layer):
    nd = len(shape)
    return pl.BlockSpec((None,) + shape, lambda *_: (layer,) + (0,) * nd, pipeline_mode=pl.Buffered(1))


def _rms(x, g):
    ms = jnp.mean(x * x, axis=-1, keepdims=True)
    return x * lax.rsqrt(ms + EPS) * g


def _ffn_body(x, g_ref, wg_ref, wu_ref, wd_ref):
    h = _rms(x, g_ref[...]).astype(BF16)
    acc = jnp.zeros(x.shape, F32)
    for c in range(D_FF // FF_CHUNK):
        sl = slice(c * FF_CHUNK, (c + 1) * FF_CHUNK)
        gate = jnp.dot(h, wg_ref[:, sl], preferred_element_type=F32)
        up = jnp.dot(h, wu_ref[:, sl], preferred_element_type=F32)
        act = (gate * jax.nn.sigmoid(gate) * up).astype(BF16)
        acc = acc + jnp.dot(act, wd_ref[sl, :], preferred_element_type=F32)
    return x + 0.5 * acc


def _ffn_mix_in_kernel(x_ref, g1_ref, wg_ref, wu_ref, wd_ref, g2_ref, wst_ref, wrest_ref,
                       xo_ref, u4_ref, up_ref, hc_ref):
    b, nc, _ = x_ref.shape
    y = _ffn_body(x_ref[...].reshape(b * nc, D_MODEL), g1_ref, wg_ref, wu_ref, wd_ref)
    xo_ref[...] = y.reshape(b, nc, D_MODEL)
    h = _rms(y, g2_ref[...]).astype(BF16)
    ut = lax.dot_general(wst_ref[...], h, (((1,), (1,)), ((), ())), preferred_element_type=F32)
    u4_ref[...] = ut.reshape(N_SSM_GROUPS, SSM_GROUP, b * nc)
    z = jnp.dot(h, wrest_ref[...], preferred_element_type=F32)
    up_ref[...] = z[:, :D_POOL].reshape(b, nc, D_POOL)
    v = z[:, D_POOL:D_POOL + D_CONV]
    g = z[:, D_POOL + D_CONV:]
    hc_ref[...] = (v * jax.nn.sigmoid(g)).reshape(b, nc, D_CONV)


def _ffn_mix_in_tm_kernel(x_ref, g1_ref, wg_ref, wu_ref, wd_ref, g2_ref, wst_ref, wrest_ref,
                          xo_ref, u4_ref, up_ref, hc_ref):
    _, nc, ts, _ = x_ref.shape
    x = jnp.swapaxes(x_ref[0], 0, 1).reshape(ts * nc, D_MODEL)
    y = _ffn_body(x, g1_ref, wg_ref, wu_ref, wd_ref)
    xo_ref[0] = y.reshape(ts, nc, D_MODEL)
    h = _rms(y, g2_ref[...]).astype(BF16)
    ut = lax.dot_general(wst_ref[...], h, (((1,), (1,)), ((), ())), preferred_element_type=F32)
    for s in range(ts):
        u4_ref[:, s] = ut[:, s * nc:(s + 1) * nc].reshape(N_SSM_GROUPS, SSM_GROUP, nc)
    z = jnp.dot(h, wrest_ref[...], preferred_element_type=F32)
    up_ref[0] = z[:, :D_POOL].reshape(ts, nc, D_POOL)
    v = z[:, D_POOL:D_POOL + D_CONV]
    g = z[:, D_POOL + D_CONV:]
    hc_ref[0] = (v * jax.nn.sigmoid(g)).reshape(ts, nc, D_CONV)


def _ffn_mix_in_tm(x_tm, g1, wg, wu, wd, g2, w_ssm_t, w_rest, *, layer, ts):
    b, nc, t, _ = x_tm.shape
    tile = lambda w: pl.BlockSpec((1, ts, nc, w), lambda i, j: (i, j, 0, 0))
    return pl.pallas_call(
        _ffn_mix_in_tm_kernel,
        out_shape=(jax.ShapeDtypeStruct((b, t, nc, D_MODEL), F32),
                   jax.ShapeDtypeStruct((N_SSM_GROUPS, t, SSM_GROUP, b * nc), F32),
                   jax.ShapeDtypeStruct((b, t, nc, D_POOL), F32),
                   jax.ShapeDtypeStruct((b, t, nc, D_CONV), F32)),
        grid=(b, t // ts),
        in_specs=[
            pl.BlockSpec((1, nc, ts, D_MODEL), lambda i, j: (i, 0, j, 0)),
            _const_spec((1, D_MODEL)),
            _layer_spec((D_MODEL, D_FF), layer),
            _layer_spec((D_MODEL, D_FF), layer),
            _layer_spec((D_FF, D_MODEL), layer),
            _const_spec((1, D_MODEL)),
            _const_spec((D_SSM, D_MODEL)),
            _const_spec((D_MODEL, D_IN - D_SSM)),
        ],
        out_specs=(tile(D_MODEL),
                   pl.BlockSpec((N_SSM_GROUPS, ts, SSM_GROUP, nc), lambda i, j: (0, j, 0, i)),
                   tile(D_POOL), tile(D_CONV)),
        compiler_params=_cparams("parallel", "parallel"),
    )(x_tm, g1, wg, wu, wd, g2, w_ssm_t, w_rest)


def _ffn_mix_in(x4, g1, wg, wu, wd, g2, w_ssm_t, w_rest, *, layer, bg):
    b, t, nc, _ = x4.shape
    step = lambda w: pl.BlockSpec((bg, None, nc, w), lambda g, i: (g, i, 0, 0))
    return pl.pallas_call(
        _ffn_mix_in_kernel,
        out_shape=(jax.ShapeDtypeStruct((b, t, nc, D_MODEL), F32),
                   jax.ShapeDtypeStruct((N_SSM_GROUPS, t, SSM_GROUP, b * nc), F32),
                   jax.ShapeDtypeStruct((b, t, nc, D_POOL), F32),
                   jax.ShapeDtypeStruct((b, t, nc, D_CONV), F32)),
        grid=(b // bg, t),
        in_specs=[
            step(D_MODEL),
            _const_spec((1, D_MODEL)),
            _layer_spec((D_MODEL, D_FF), layer),
            _layer_spec((D_MODEL, D_FF), layer),
            _layer_spec((D_FF, D_MODEL), layer),
            _const_spec((1, D_MODEL)),
            _const_spec((D_SSM, D_MODEL)),
            _const_spec((D_MODEL, D_IN - D_SSM)),
        ],
        out_specs=(step(D_MODEL),
                   pl.BlockSpec((N_SSM_GROUPS, None, SSM_GROUP, bg * nc), lambda g, i: (0, i, 0, g)),
                   step(D_POOL), step(D_CONV)),
        compiler_params=_cparams("parallel", "parallel"),
    )(x4, g1, wg, wu, wd, g2, w_ssm_t, w_rest)


def _s5_prep_kernel(lr2_ref, li2_ref, ldt_ref, br2_ref, bi2_ref, lrc_ref, lic_ref, crt_ref, cit_ref,
                    zt_ref, wst_ref, gt_ref, cp_ref):
    hi = lax.Precision.HIGHEST
    lane = lax.broadcasted_iota(jnp.int32, (1, LANES), 1)
    first = lane < SSM_STATE

    lr2 = lr2_ref[0]
    li2 = li2_ref[0]
    dt = jnp.exp(ldt_ref[0])
    rho = lr2 * dt
    th = li2 * dt

    def pw_row(e):
        mag = jnp.exp(e * rho)
        return mag * jnp.cos(e * th), mag * jnp.sin(e * th)

    one = jnp.ones((1, 1), F32)
    a_r, a_i = pw_row(one)
    den = lr2 * lr2 + li2 * li2
    z_r = ((a_r - 1.0) * lr2 + a_i * li2) / den
    z_i = (a_i * lr2 - (a_r - 1.0) * li2) / den
    br2 = br2_ref[0]
    bi2 = bi2_ref[0]
    b1 = jnp.where(first, br2, bi2)
    b2 = jnp.where(first, -bi2, br2)
    y1 = z_r * b1 + z_i * b2
    y2 = z_r * b2 - z_i * b1

    e_lo = (S5_SUB - 1 - lax.broadcasted_iota(jnp.int32, (S5_SUB, 1), 0)).astype(F32)
    lo_r, lo_i = pw_row(e_lo)
    lob1 = jnp.concatenate([lo_r[i:i + 1] * y1 + lo_i[i:i + 1] * y2 for i in range(S5_SUB)], axis=0)
    lob2 = jnp.concatenate([lo_r[i:i + 1] * y2 - lo_i[i:i + 1] * y1 for i in range(S5_SUB)], axis=0)
    e_hi = (S5_SUB * (S5_NB - 1 - lax.broadcasted_iota(jnp.int32, (S5_NB, 1), 0))).astype(F32)
    hi_r, hi_i = pw_row(e_hi)
    pb = jnp.concatenate([hi_r[m:m + 1] * lob1 + hi_i[m:m + 1] * lob2 for m in range(S5_NB)], axis=0)
    wst_ref[0] = pb.astype(BF16)

    lrc = lrc_ref[0]
    lic = lic_ref[0]
    rho_c = lrc * dt
    th_c = lic * dt
    mag_c = jnp.exp(rho_c)
    ac_r = mag_c * jnp.cos(th_c)
    ac_i = mag_c * jnp.sin(th_c)
    rc_r, rc_i = [], []
    for half in range(MXU_TILE // LANES):
        e = (lane // SSM_GROUP + half * (LANES // SSM_GROUP)).astype(F32)
        mag = jnp.exp(e * rho_c)
        t_r = mag * jnp.cos(e * th_c)
        t_i = mag * jnp.sin(e * th_c)
        c_r = crt_ref[0, :, half * LANES:(half + 1) * LANES]
        c_i = cit_ref[0, :, half * LANES:(half + 1) * LANES]
        rc_r.append(t_r * c_r - t_i * c_i)
        rc_i.append(t_r * c_i + t_i * c_r)
    ct_stack = jnp.concatenate([crt_ref[0], -cit_ref[0]], axis=0)

    def csq(v_r, v_i):
        return v_r * v_r - v_i * v_i, 2.0 * v_r * v_i

    a16_r, a16_i = ac_r, ac_i
    for _ in range(S5_SUB.bit_length() - 1):
        a16_r, a16_i = csq(a16_r, a16_i)
    g_r, g_i = ac_r, ac_i
    for j in range(S5_NB):
        for half in range(MXU_TILE // LANES):
            lo = j * MXU_TILE + half * LANES
            gt_ref[0, :SSM_STATE, lo:lo + LANES] = (g_r * rc_r[half] - g_i * rc_i[half]).astype(BF16)
            gt_ref[0, SSM_STATE:, lo:lo + LANES] = (-(g_r * rc_i[half] + g_i * rc_r[half])).astype(BF16)
        g_r, g_i = g_r * a16_r - g_i * a16_i, g_r * a16_i + g_i * a16_r

    w_r, w_i = a16_r, a16_i
    for _ in range(S5_NB.bit_length() - 1):
        w_r, w_i = csq(w_r, w_i)
    for r in range(S5_MAX_SCAN_STEPS):
        cp_ref[0, 2 * r * SSM_STATE:(2 * r + 1) * SSM_STATE, :] = w_r
        cp_ref[0, (2 * r + 1) * SSM_STATE:(2 * r + 2) * SSM_STATE, :] = w_i
        w_r, w_i = csq(w_r, w_i)

    off_rows = (S5_CHUNK - S5_SUB) * SSM_GROUP
    if off_rows:
        rc_stack = jnp.concatenate([jnp.concatenate(rc_r, axis=1), -jnp.concatenate(rc_i, axis=1)], axis=0)
        lo_row = (S5_SUB - 1) * SSM_GROUP
        zt_off = jnp.dot(pb[lo_row:lo_row + off_rows], rc_stack, precision=hi, preferred_element_type=F32)
        zt_ref[0, :off_rows, :] = zt_off.astype(BF16)
    kall = jnp.dot(pb[off_rows:], ct_stack, precision=hi, preferred_element_type=F32)
    colblk = lax.broadcasted_iota(jnp.int32, (1, MXU_TILE), 1) // SSM_GROUP
    zd = jnp.zeros((MXU_TILE, MXU_TILE), F32)
    for jj in range(S5_SUB):
        sh = (S5_SUB - 1 - jj) * SSM_GROUP
        shifted = kall if sh == 0 else jnp.concatenate(
            [kall[sh:], jnp.zeros((sh, MXU_TILE), F32)], axis=0)
        zd = jnp.where(colblk == jj, shifted, zd)
    zt_ref[0, off_rows:, :] = zd.astype(BF16)


def _s5_prep(lam_re, lam_im, log_dt, b_re, b_im, c_re, c_im):
    g = N_SSM_GROUPS
    lr2 = jnp.tile(lam_re, (1, 2))[:, None, :]
    li2 = jnp.tile(lam_im, (1, 2))[:, None, :]
    ldt = jnp.broadcast_to(log_dt[:, None, None], (g, 1, LANES))
    br2 = jnp.tile(jnp.swapaxes(b_re, 1, 2), (1, 1, 2))
    bi2 = jnp.tile(jnp.swapaxes(b_im, 1, 2), (1, 1, 2))
    lrc = jnp.broadcast_to(lam_re[:, :, None], (g, SSM_STATE, LANES))
    lic = jnp.broadcast_to(lam_im[:, :, None], (g, SSM_STATE, LANES))
    crt = jnp.tile(jnp.swapaxes(c_re, 1, 2), (1, 1, S5_SUB))
    cit = jnp.tile(jnp.swapaxes(c_im, 1, 2), (1, 1, S5_SUB))

    def spec(*shape):
        return pl.BlockSpec((1,) + shape, lambda i: (i, 0, 0))

    cp_rows = 2 * S5_MAX_SCAN_STEPS * SSM_STATE
    return pl.pallas_call(
        _s5_prep_kernel,
        out_shape=(jax.ShapeDtypeStruct((g, S5_ROWS, MXU_TILE), BF16),
                   jax.ShapeDtypeStruct((g, S5_ROWS, LANES), BF16),
                   jax.ShapeDtypeStruct((g, LANES, S5_ROWS), BF16),
                   jax.ShapeDtypeStruct((g, cp_rows, LANES), F32)),
        grid=(g,),
        in_specs=[spec(1, LANES), spec(1, LANES), spec(1, LANES),
                  spec(SSM_GROUP, LANES), spec(SSM_GROUP, LANES),
                  spec(SSM_STATE, LANES), spec(SSM_STATE, LANES),
                  spec(SSM_STATE, MXU_TILE), spec(SSM_STATE, MXU_TILE)],
        out_specs=(spec(S5_ROWS, MXU_TILE), spec(S5_ROWS, LANES), spec(LANES, S5_ROWS),
                   spec(cp_rows, LANES)),
        compiler_params=_cparams("parallel"),
    )(lr2, li2, ldt, br2, bi2, lrc, lic, crt, cit)


def _s5_main_kernel(u_ref, zw_ref, wst_ref, g_ref, cp_ref, d_ref, o_ref, *, n_chunks):
    ncols = u_ref.shape[-1]
    u = u_ref[0].reshape(S5_ROWS, ncols).astype(BF16)
    state = jnp.dot(wst_ref[0], u, preferred_element_type=F32)
    s_r, s_i = state[:SSM_STATE], state[SSM_STATE:]
    chunk = lax.broadcasted_iota(jnp.int32, (1, ncols), 1) % n_chunks
    reps = ncols // LANES if ncols > LANES else 1

    def wide(v):
        return jnp.concatenate([v] * reps, axis=1)[:, :ncols]

    r = 0
    while (1 << r) < n_chunks:
        s = 1 << r
        p_r = jnp.where(chunk >= s, pltpu.roll(s_r, s, 1), 0.0)
        p_i = jnp.where(chunk >= s, pltpu.roll(s_i, s, 1), 0.0)
        w_r = wide(cp_ref[0, 2 * r * SSM_STATE:(2 * r + 1) * SSM_STATE, :])
        w_i = wide(cp_ref[0, (2 * r + 1) * SSM_STATE:(2 * r + 2) * SSM_STATE, :])
        s_r, s_i = s_r + w_r * p_r - w_i * p_i, s_i + w_r * p_i + w_i * p_r
        r += 1
    c_r = jnp.where(chunk >= 1, pltpu.roll(s_r, 1, 1), 0.0)
    c_i = jnp.where(chunk >= 1, pltpu.roll(s_i, 1, 1), 0.0)
    carry = jnp.concatenate([c_r, c_i], axis=0).astype(BF16)
    for j in range(S5_NB):
        k = (j + 1) * MXU_TILE
        y = jnp.dot(zw_ref[0, :, (S5_NB - 1 - j) * MXU_TILE:], u[:k], preferred_element_type=F32)
        y = y + jnp.dot(g_ref[0, j * MXU_TILE:(j + 1) * MXU_TILE, :], carry, preferred_element_type=F32)
        rows = slice(j * MXU_TILE, (j + 1) * MXU_TILE)
        y = y + d_ref[0, rows] * u_ref[0].reshape(S5_ROWS, ncols)[rows]
        for s in range(S5_SUB):
            for b in range(ncols // n_chunks):
                o_ref[0, j * S5_SUB + s, b] = y[s * SSM_GROUP:(s + 1) * SSM_GROUP,
                                                b * n_chunks:(b + 1) * n_chunks]


def _s5_main(u4, zw, wst, gm, cp, d_rows, *, n_chunks):
    g, t, _, ncols = u4.shape
    cp_rows = cp.shape[1]

    def spec(*shape):
        return pl.BlockSpec((1,) + shape, lambda i: (i,) + (0,) * len(shape))

    return pl.pallas_call(
        functools.partial(_s5_main_kernel, n_chunks=n_chunks),
        out_shape=jax.ShapeDtypeStruct((g, t, ncols // n_chunks, SSM_GROUP, n_chunks), F32),
        grid=(g,),
        in_specs=[spec(t, SSM_GROUP, ncols), spec(MXU_TILE, S5_ROWS), spec(LANES, S5_ROWS),
                  spec(S5_ROWS, LANES), spec(cp_rows, LANES), spec(S5_ROWS, 1)],
        out_specs=spec(t, ncols // n_chunks, SSM_GROUP, n_chunks),
        compiler_params=_cparams("parallel"),
    )(u4, zw, wst, gm, cp, d_rows)


def _kv_kernel(m_ref, g_ref, wk_ref, wv_ref, k_ref, v_ref):
    m = _rms(m_ref[...], g_ref[...]).astype(BF16)
    k_ref[...] = jnp.dot(m, wk_ref[...], preferred_element_type=F32).astype(BF16)
    v_ref[...] = jnp.dot(m, wv_ref[...], preferred_element_type=F32).astype(BF16)


def _kv(mem2d, norm_g, wk, wv, *, layer, tm):
    n = mem2d.shape[0]
    row = pl.BlockSpec((tm, D_MODEL), lambda i: (i, 0))
    return pl.pallas_call(
        _kv_kernel,
        out_shape=(jax.ShapeDtypeStruct((n, D_MODEL), BF16), jax.ShapeDtypeStruct((n, D_MODEL), BF16)),
        grid=(n // tm,),
        in_specs=[row, _const_spec((1, D_MODEL)), _layer_spec((D_MODEL, D_MODEL), layer),
                  _layer_spec((D_MODEL, D_MODEL), layer)],
        out_specs=(row, row),
        compiler_params=_cparams("parallel"),
    )(mem2d, norm_g, wk, wv)


def _fill_history(ext_ref, src_ref, pad, chunk):
    t = src_ref.shape[1]
    ext_ref[pad:] = src_ref[0]
    for e in range(pad):
        back = e // t + 1
        v = src_ref[0, t - 1 - e % t]
        ext_ref[pad - 1 - e] = jnp.where(chunk >= back, pltpu.roll(v, back, 0), 0.0)


def _pool_conv_tile(uext_ref, hext_ref, p_ref, c_ref, pw_ref, ps_ref, cw_ref, cb_ref, lg_ref, lb_ref,
                    step0, ts, t):
    nc = uext_ref.shape[1]
    chunk = lax.broadcasted_iota(jnp.int32, (nc, 1), 0)
    low = lax.broadcasted_iota(jnp.int32, (1, LANES), 1) < POOL_GROUP
    first_tap = CONV_PAD - (CONV_WIDTH - 1)
    groups_per_tile = LANES // POOL_GROUP

    def one_step(s, carry):
        i = step0 + s
        rows = pl.ds(pl.multiple_of(s * nc, nc), nc)
        pos = (chunk * t + i + 1).astype(F32)
        for lt in range(D_POOL // LANES):
            lanes = slice(lt * LANES, (lt + 1) * LANES)
            w_lo, w_hi = POOL_WINDOWS[groups_per_tile * lt:groups_per_tile * (lt + 1)]
            cur = uext_ref[POOL_PAD + i, :, lanes]
            run = cur
            saved = {}
            for d in range(1, w_hi):
                run = run + uext_ref[POOL_PAD + i - d, :, lanes]
                if d + 1 in (w_lo, w_hi):
                    saved[d + 1] = run
            total = jnp.where(low, saved[w_lo], saved[w_hi])
            count = jnp.minimum(pos, jnp.where(low, float(w_lo), float(w_hi)))
            p_ref[rows, lanes] = total / count - cur
        for lt in range(D_CONV // LANES):
            lanes = slice(lt * LANES, (lt + 1) * LANES)
            acc = jnp.zeros((nc, LANES), F32) + cb_ref[:, lanes]
            for k in range(CONV_WIDTH):
                acc = acc + cw_ref[k:k + 1, lanes] * hext_ref[i + first_tap + k, :, lanes]
            c_ref[rows, lanes] = acc
        return carry

    lax.fori_loop(0, ts, one_step, 0)
    y_pool = jnp.dot(p_ref[...].astype(BF16), pw_ref[...], preferred_element_type=F32) * ps_ref[...]
    conv = c_ref[...]
    mu = jnp.mean(conv, axis=-1, keepdims=True)
    cen = conv - mu
    var = jnp.mean(cen * cen, axis=-1, keepdims=True)
    hn = cen * lax.rsqrt(var + EPS) * lg_ref[...] + lb_ref[...]
    return jnp.concatenate([y_pool, hn * jax.nn.sigmoid(hn)], axis=1).astype(BF16)


def _mix_attn_kernel(x_ref, y4_ref, up_ref, hc_ref, k_ref, v_ref,
                     pw_ref, ps_ref, cw_ref, cb_ref, lg_ref, lb_ref,
                     wglut_ref, wmix_ref, ga_ref, wq_ref, wo_ref,
                     o_ref, uext_ref, hext_ref, p_ref, c_ref):
    _, ts, nc, _ = x_ref.shape
    t = up_ref.shape[1]
    j = pl.program_id(1)
    chunk = lax.broadcasted_iota(jnp.int32, (nc, 1), 0)

    @pl.when(j == 0)
    def _():
        _fill_history(uext_ref, up_ref, POOL_PAD, chunk)
        _fill_history(hext_ref, hc_ref, CONV_PAD, chunk)

    y_pc = _pool_conv_tile(uext_ref, hext_ref, p_ref, c_ref, pw_ref, ps_ref, cw_ref, cb_ref, lg_ref, lb_ref,
                           j * ts, ts, t)

    yt = jnp.concatenate([y4_ref[:, s].reshape(D_SSM, nc) for s in range(ts)], axis=1)
    yt = jax.nn.gelu(yt, approximate=True)
    gate = jax.nn.sigmoid(jnp.dot(wglut_ref[...], yt.astype(BF16), preferred_element_type=F32))
    y_ssm = (yt * gate).T.astype(BF16)

    x = x_ref[0].reshape(ts * nc, D_MODEL)
    y_mix = jnp.concatenate([y_ssm, y_pc], axis=1)
    x = x + jnp.dot(y_mix, wmix_ref[...], preferred_element_type=F32)

    h = _rms(x, ga_ref[...]).astype(BF16)
    q = jnp.dot(h, wq_ref[...], preferred_element_type=F32) * (XHEAD_DIM ** -0.5)
    heads = []
    for hd in range(N_XHEADS):
        sl = slice(hd * XHEAD_DIM, (hd + 1) * XHEAD_DIM)
        sc = lax.dot_general(q[:, sl].astype(BF16), k_ref[0, :, sl], (((1,), (1,)), ((), ())),
                             preferred_element_type=F32)
        pr = jnp.exp(sc - jnp.max(sc, axis=-1, keepdims=True))
        l = jnp.sum(pr, axis=-1, keepdims=True)
        heads.append(jnp.dot(pr.astype(BF16), v_ref[0, :, sl], preferred_element_type=F32) / l)
    o = jnp.concatenate(heads, axis=-1).astype(BF16)
    x = x + jnp.dot(o, wo_ref[...], preferred_element_type=F32)
    o_ref[0] = x.reshape(ts, nc, D_MODEL)


def _mix_attn(x4, y4, up4, hc4, k3d, v3d, pool_w_bd, pool_scale, conv_w, conv_b, ln_g, ln_b,
              w_glu_t, w_mix, ga, wq, wo, *, layer, ts):
    b, t, nc, _ = x4.shape
    xspec = pl.BlockSpec((1, ts, nc, D_MODEL), lambda i, j: (i, j, 0, 0))
    chan = pl.BlockSpec((N_SSM_GROUPS, ts, None, SSM_GROUP, nc), lambda i, j: (0, j, i, 0, 0))
    whole = lambda w: pl.BlockSpec((1, t, nc, w), lambda i, j: (i, 0, 0, 0))
    mem = pl.BlockSpec((1, MEM_LEN, D_MODEL), lambda i, j: (i, 0, 0))
    return pl.pallas_call(
        _mix_attn_kernel,
        out_shape=jax.ShapeDtypeStruct((b, t, nc, D_MODEL), F32),
        grid=(b, t // ts),
        in_specs=[
            xspec, chan, whole(D_POOL), whole(D_CONV), mem, mem,
            _const_spec((D_POOL, D_POOL)), _const_spec((1, D_POOL)),
            _const_spec((CONV_WIDTH, D_CONV)), _const_spec((1, D_CONV)),
            _const_spec((1, D_CONV)), _const_spec((1, D_CONV)),
            _const_spec((D_SSM, D_SSM)),
            _layer_spec((D_MODEL, D_MODEL), layer),
            _const_spec((1, D_MODEL)), _layer_spec((D_MODEL, D_MODEL), layer),
            _layer_spec((D_MODEL, D_MODEL), layer),
        ],
        out_specs=xspec,
        scratch_shapes=[pltpu.VMEM((POOL_PAD + t, nc, D_POOL), F32),
                        pltpu.VMEM((CONV_PAD + t, nc, D_CONV), F32),
                        pltpu.VMEM((ts * nc, D_POOL), F32),
                        pltpu.VMEM((ts * nc, D_CONV), F32)],
        compiler_params=_cparams("parallel", "arbitrary"),
    )(x4, y4, up4, hc4, k3d, v3d, pool_w_bd, pool_scale, conv_w, conv_b, ln_g, ln_b,
      w_glu_t, w_mix, ga, wq, wo)


def _ffn_kernel(x_ref, g_ref, wg_ref, wu_ref, wd_ref, o_ref):
    o_ref[...] = _ffn_body(x_ref[...], g_ref, wg_ref, wu_ref, wd_ref)


def _ffn(x2d, norm_g, wg, wu, wd, *, layer, tm):
    n = x2d.shape[0]
    row = pl.BlockSpec((tm, D_MODEL), lambda i: (i, 0))
    return pl.pallas_call(
        _ffn_kernel,
        out_shape=jax.ShapeDtypeStruct((n, D_MODEL), F32),
        grid=(n // tm,),
        in_specs=[row, _const_spec((1, D_MODEL)), _layer_spec((D_MODEL, D_FF), layer),
                  _layer_spec((D_MODEL, D_FF), layer), _layer_spec((D_FF, D_MODEL), layer)],
        out_specs=row,
        compiler_params=_cparams("parallel"),
    )(x2d, norm_g, wg, wu, wd)


def _ffn_out_kernel(x_ref, g_ref, wg_ref, wu_ref, wd_ref, fg_ref, o_ref):
    _, ts, nc, _ = x_ref.shape
    y = _ffn_body(x_ref[0].reshape(ts * nc, D_MODEL), g_ref, wg_ref, wu_ref, wd_ref)
    y = _rms(y, fg_ref[...]).reshape(ts, nc, D_MODEL)
    o_ref[0] = jnp.swapaxes(y, 0, 1)


def _ffn_out(x4, norm_g, wg, wu, wd, final_g, *, layer, ts):
    b, t, nc, _ = x4.shape
    return pl.pallas_call(
        _ffn_out_kernel,
        out_shape=jax.ShapeDtypeStruct((b, nc, t, D_MODEL), F32),
        grid=(b, t // ts),
        in_specs=[pl.BlockSpec((1, ts, nc, D_MODEL), lambda i, j: (i, j, 0, 0)),
                  _const_spec((1, D_MODEL)), _layer_spec((D_MODEL, D_FF), layer),
                  _layer_spec((D_MODEL, D_FF), layer), _layer_spec((D_FF, D_MODEL), layer),
                  _const_spec((1, D_MODEL))],
        out_specs=pl.BlockSpec((1, nc, ts, D_MODEL), lambda i, j: (i, 0, j, 0)),
        compiler_params=_cparams("parallel", "parallel"),
    )(x4, norm_g, wg, wu, wd, final_g)


def _tile(n, pref):
    t = max(1, min(n, pref))
    assert n % t == 0, (n, t)
    return t


def kernel(x, mem, ffn1_norm, ffn1_w_gate, ffn1_w_up, ffn1_w_down, mix_norm, w_in, w_out, ssm_lambda_re, ssm_lambda_im, ssm_log_dt, ssm_b_re, ssm_b_im, ssm_c_re, ssm_c_im, ssm_d, ssm_w_glu, pool_w, pool_scale, conv_w, conv_b, conv_ln_g, conv_ln_b, xattn_norm, mem_norm, xattn_wq, xattn_wk, xattn_wv, xattn_wo, ffn2_norm, ffn2_w_gate, ffn2_w_up, ffn2_w_down, final_norm):
    bsz, seq, _ = x.shape
    depth = w_in.shape[0]
    n = bsz * seq
    assert seq % S5_CHUNK == 0
    n_chunks = seq // S5_CHUNK
    assert n_chunks & (n_chunks - 1) == 0 and n_chunks <= (1 << S5_MAX_SCAN_STEPS)
    bg = _tile(bsz, FFN_TOKEN_TILE // n_chunks)
    ts = _tile(S5_CHUNK, TOKEN_TILE // n_chunks)
    to = _tile(S5_CHUNK, SUBLANES)
    tm = _tile(n, FFN_TOKEN_TILE)
    tkv = _tile(bsz * MEM_LEN, TOKEN_TILE)
    bf = lambda w: w.astype(BF16)
    row = lambda v: v.reshape(1, -1)

    mem2d = mem.reshape(bsz * MEM_LEN, D_MODEL)
    f1g, f1u, f1d = bf(ffn1_w_gate), bf(ffn1_w_up), bf(ffn1_w_down)
    f2g, f2u, f2d = bf(ffn2_w_gate), bf(ffn2_w_up), bf(ffn2_w_down)
    wq, wk, wv, wo = bf(xattn_wq), bf(xattn_wk), bf(xattn_wv), bf(xattn_wo)
    w_mix = bf(w_out)
    for l in range(depth):
        in_args = (row(ffn1_norm[l]), f1g, f1u, f1d, row(mix_norm[l]), bf(w_in[l, :, :D_SSM].T),
                   bf(w_in[l, :, D_SSM:]))
        if l == 0:
            x4, u4, up4, hc4 = _ffn_mix_in_tm(x.reshape(bsz, n_chunks, S5_CHUNK, D_MODEL), *in_args,
                                              layer=l, ts=to)
        else:
            x4, u4, up4, hc4 = _ffn_mix_in(x4, *in_args, layer=l, bg=bg)

        zt, wst_t, gt, cp = _s5_prep(ssm_lambda_re[l], ssm_lambda_im[l], ssm_log_dt[l], ssm_b_re[l],
                                     ssm_b_im[l], ssm_c_re[l], ssm_c_im[l])
        d_rows = jnp.tile(ssm_d[l].reshape(N_SSM_GROUPS, 1, SSM_GROUP), (1, S5_CHUNK, 1))
        y4 = _s5_main(u4, jnp.swapaxes(zt, 1, 2), jnp.swapaxes(wst_t, 1, 2), jnp.swapaxes(gt, 1, 2), cp,
                      d_rows.reshape(N_SSM_GROUPS, S5_ROWS, 1), n_chunks=n_chunks)

        k2d, v2d = _kv(mem2d, row(mem_norm[l]), wk, wv, layer=l, tm=tkv)
        pool_bd = jax.scipy.linalg.block_diag(*[pool_w[l, g] for g in range(len(POOL_WINDOWS))])
        x4 = _mix_attn(x4, y4, up4, hc4, k2d.reshape(bsz, MEM_LEN, D_MODEL),
                       v2d.reshape(bsz, MEM_LEN, D_MODEL), bf(pool_bd), row(pool_scale[l]), conv_w[l],
                       row(conv_b[l]), row(conv_ln_g[l]), row(conv_ln_b[l]),
                       bf(ssm_w_glu[l].T), w_mix,
                       row(xattn_norm[l]), wq, wo, layer=l, ts=ts)

        if l == depth - 1:
            out = _ffn_out(x4, row(ffn2_norm[l]), f2g, f2u, f2d, row(final_norm), layer=l, ts=to)
            return out.reshape(bsz, seq, D_MODEL)
        x2d = _ffn(x4.reshape(n, D_MODEL), row(ffn2_norm[l]), f2g, f2u, f2d, layer=l, tm=tm)
        x4 = x2d.reshape(bsz, S5_CHUNK, n_chunks, D_MODEL)
```
